```python
import jax, jax.numpy as jnp
from jax import lax
import numpy as np

D_MODEL = 1024
BATCH = 8
SEQ = 4096
DEPTH = 2
DEC_BATCH = 8
DEC_SEQ = 2048
PAST_LEN = 128

GRID_W = 64
NORM_EPS = 1e-6
GLA_HEADS = 4
GLA_DK = 64
GLA_DV = 128
GLA_RANK = 16
GLA_GATE_NORM = 16.0
GLA_CHUNK = 64
FNET_GROUPS = 4
FNET_GW = 128
ATTN_HEADS = 8
KV_HEADS = 2
HEAD_DIM = 128
ROPE_THETA = 10000.0
Q_BLOCK = 128
FFN_HIDDEN = -(-8 * D_MODEL // (3 * 256)) * 256
AB_IN = 2 * GLA_HEADS * GLA_DK + 2 * GLA_HEADS * GLA_DV + 2 * GLA_RANK + FNET_GROUPS * FNET_GW
AB_OUT = GLA_HEADS * GLA_DV + FNET_GROUPS * FNET_GW
C_IN = (ATTN_HEADS + 2 * KV_HEADS) * HEAD_DIM

kernel_name = 'hybrid_gla_fnet_axial_gqa_encoder'


def rms_norm(x, gain):
    xf = x.astype(jnp.float32)
    y = xf * lax.rsqrt(jnp.mean(xf * xf, axis=-1, keepdims=True) + NORM_EPS)
    return (y * gain.astype(jnp.float32)).astype(x.dtype)


def gla_direction(q, k, v, g):
    b_, s_, h_, dk = q.shape
    dv = v.shape[-1]
    n = s_ // GLA_CHUNK
    qc = q.reshape(b_, n, GLA_CHUNK, h_, dk)
    kc = k.reshape(b_, n, GLA_CHUNK, h_, dk)
    vc = v.reshape(b_, n, GLA_CHUNK, h_, dv)
    gc = g.reshape(b_, n, GLA_CHUNK, h_, dk)
    bcum = jnp.cumsum(gc, axis=2)
    q_in = qc * jnp.exp(bcum)
    k_in = kc * jnp.exp(-bcum)
    lower = jnp.tril(jnp.ones((GLA_CHUNK, GLA_CHUNK), dtype=bool))
    att = jnp.where(lower, jnp.einsum('bnihd,bnjhd->bnhij', q_in, k_in), 0.0)
    o_intra = jnp.einsum('bnhij,bnjhe->bnihe', att, vc)
    b_last = bcum[:, :, -1]
    d_state = jnp.einsum('bnjhd,bnjhe->bnhde', kc * jnp.exp(b_last[:, :, None] - bcum), vc)

    def step(state, inp):
        ds, dec = inp
        return dec[..., None] * state + ds, state

    init = jnp.zeros((b_, h_, dk, dv), jnp.float32)
    _, s_prev = lax.scan(step, init, (jnp.moveaxis(d_state, 1, 0), jnp.moveaxis(jnp.exp(b_last), 1, 0)))
    s_prev = jnp.moveaxis(s_prev, 0, 1)
    o_inter = jnp.einsum('bnihd,bnhde->bnihe', q_in, s_prev)
    return (o_intra + o_inter).reshape(b_, s_, h_, dv)


def ab_mixer(x, norm_g, w_in, up_f, bias_f, up_b, bias_b, out_g, w_out):
    b_, s_, _ = x.shape
    f32 = jnp.float32
    h = rms_norm(x, norm_g)
    sizes = [GLA_HEADS * GLA_DK, GLA_HEADS * GLA_DK, GLA_HEADS * GLA_DV, GLA_HEADS * GLA_DV, GLA_RANK, GLA_RANK]
    cuts = [int(c) for c in np.cumsum(sizes)]
    q, k, v, r, lf, lb, f = jnp.split(h @ w_in, cuts, axis=-1)
    q = q.astype(f32).reshape(b_, s_, GLA_HEADS, GLA_DK) * (GLA_DK ** -0.5)
    k = k.astype(f32).reshape(b_, s_, GLA_HEADS, GLA_DK)
    v = v.astype(f32).reshape(b_, s_, GLA_HEADS, GLA_DV)
    g_f = (jax.nn.log_sigmoid(lf.astype(f32) @ up_f.astype(f32) + bias_f.astype(f32)) / GLA_GATE_NORM).reshape(b_, s_, GLA_HEADS, GLA_DK)
    g_b = (jax.nn.log_sigmoid(lb.astype(f32) @ up_b.astype(f32) + bias_b.astype(f32)) / GLA_GATE_NORM).reshape(b_, s_, GLA_HEADS, GLA_DK)
    o_f = gla_direction(q, k, v, g_f)
    o_b = jnp.flip(gla_direction(jnp.flip(q, 1), jnp.flip(k, 1), jnp.flip(v, 1), jnp.flip(g_b, 1)), 1)
    o = rms_norm(o_f + o_b, out_g).reshape(b_, s_, GLA_HEADS * GLA_DV) * jax.nn.silu(r.astype(f32))
    fg = f.astype(f32).reshape(b_, s_, FNET_GROUPS, FNET_GW)
    fo = jnp.real(jnp.fft.fft2(fg, axes=(1, 3), norm='ortho')).reshape(b_, s_, FNET_GROUPS * FNET_GW)
    mixed = jnp.concatenate([o, fo], axis=-1).astype(x.dtype) @ w_out
    return x + mixed


def axial_angles(s_):
    rows = s_ // GRID_W
    row_idx = jnp.repeat(jnp.arange(rows), GRID_W).astype(jnp.float32)
    col_idx = jnp.tile(jnp.arange(GRID_W), rows).astype(jnp.float32)
    half = HEAD_DIM // 2
    inv = ROPE_THETA ** (-jnp.arange(0, half, 2, dtype=jnp.float32) / half)
    return row_idx[:, None] * inv[None, :], col_idx[:, None] * inv[None, :]


def rope_rotate(x, ang):
    m = ang.shape[-1]
    x1, x2 = x[..., :m], x[..., m:]
    c = jnp.cos(ang)[:, None, :]
    s = jnp.sin(ang)[:, None, :]
    return jnp.concatenate([x1 * c - x2 * s, x2 * c + x1 * s], axis=-1)


def apply_axial_rope(x, ang_row, ang_col):
    half = HEAD_DIM // 2
    xf = x.astype(jnp.float32)
    out = jnp.concatenate([rope_rotate(xf[..., :half], ang_row), rope_rotate(xf[..., half:], ang_col)], axis=-1)
    return out.astype(x.dtype)


def c_mixer(x, norm_g, w_in, qn_g, kn_g, w_out):
    b_, s_, _ = x.shape
    h = rms_norm(x, norm_g)
    q, k, v = jnp.split(h @ w_in, [ATTN_HEADS * HEAD_DIM, (ATTN_HEADS + KV_HEADS) * HEAD_DIM], axis=-1)
    q = rms_norm(q.reshape(b_, s_, ATTN_HEADS, HEAD_DIM), qn_g)
    k = rms_norm(k.reshape(b_, s_, KV_HEADS, HEAD_DIM), kn_g)
    v = v.reshape(b_, s_, KV_HEADS, HEAD_DIM)
    ang_row, ang_col = axial_angles(s_)
    q = apply_axial_rope(q, ang_row, ang_col)
    k = apply_axial_rope(k, ang_row, ang_col)
    group = ATTN_HEADS // KV_HEADS
    nblk = s_ // Q_BLOCK
    qb = jnp.moveaxis(q.reshape(b_, nblk, Q_BLOCK, KV_HEADS, group, HEAD_DIM), 1, 0)
    scale = HEAD_DIM ** -0.5

    def block(qblk):
        sc = jnp.einsum('bqkgd,bskd->bkgqs', qblk, k).astype(jnp.float32) * scale
        pr = jax.nn.softmax(sc, axis=-1).astype(v.dtype)
        return jnp.einsum('bkgqs,bskd->bqkgd', pr, v)

    o = lax.map(block, qb)
    o = jnp.moveaxis(o, 0, 1).reshape(b_, s_, ATTN_HEADS * HEAD_DIM)
    return x + o @ w_out


def swiglu_ffn(x, norm_g, w_gate, w_up, w_down):
    h = rms_norm(x, norm_g)
    return x + (jax.nn.silu(h @ w_gate) * (h @ w_up)) @ w_down


def trunk(x, ab_norm, ab_w_in, gla_up_f, gla_bias_f, gla_up_b, gla_bias_b, gla_out_norm, ab_w_out,
          c_norm, c_w_in, c_q_norm, c_k_norm, c_w_out, ffn_norm, ffn_w_gate, ffn_w_up, ffn_w_down):
    for i in range(DEPTH):
        j = i // 2
        if i % 2 == 0:
            x = ab_mixer(x, ab_norm[j], ab_w_in[j], gla_up_f[j], gla_bias_f[j], gla_up_b[j], gla_bias_b[j],
                         gla_out_norm[j], ab_w_out[j])
        else:
            x = c_mixer(x, c_norm[j], c_w_in[j], c_q_norm[j], c_k_norm[j], c_w_out[j])
        x = swiglu_ffn(x, ffn_norm[i], ffn_w_gate[i], ffn_w_up[i], ffn_w_down[i])
    return x


def setup_inputs(seed: int = 0) -> dict:
    key = jax.random.key(seed)
    ks = jax.random.split(key, 24)
    ne = (DEPTH + 1) // 2
    no = DEPTH // 2
    f32 = jnp.float32

    def nrm(k, shape, scale):
        return jax.random.normal(k, shape, f32) * scale

    def gain(k, shape):
        return 1.0 + 0.01 * jax.random.normal(k, shape, f32)

    return {
        'x_prompt': nrm(ks[0], (BATCH, SEQ, D_MODEL), 1.0),
        'x_sample': nrm(ks[1], (DEC_BATCH, DEC_SEQ, D_MODEL), 1.0),
        'ab_norm': gain(ks[2], (ne, D_MODEL)),
        'ab_w_in': nrm(ks[3], (ne, D_MODEL, AB_IN), D_MODEL ** -0.5),
        'gla_up_f': nrm(ks[4], (ne, GLA_RANK, GLA_HEADS * GLA_DK), GLA_RANK ** -0.5),
        'gla_bias_f': nrm(ks[5], (ne, GLA_HEADS * GLA_DK), 0.1),
        'gla_up_b': nrm(ks[6], (ne, GLA_RANK, GLA_HEADS * GLA_DK), GLA_RANK ** -0.5),
        'gla_bias_b': nrm(ks[7], (ne, GLA_HEADS * GLA_DK), 0.1),
        'gla_out_norm': gain(ks[8], (ne, GLA_DV)),
        'ab_w_out': nrm(ks[9], (ne, AB_OUT, D_MODEL), AB_OUT ** -0.5),
        'c_norm': gain(ks[10], (no, D_MODEL)),
        'c_w_in': nrm(ks[11], (no, D_MODEL, C_IN), D_MODEL ** -0.5),
        'c_q_norm': gain(ks[12], (no, HEAD_DIM)),
        'c_k_norm': gain(ks[13], (no, HEAD_DIM)),
        'c_w_out': nrm(ks[14], (no, ATTN_HEADS * HEAD_DIM, D_MODEL), (ATTN_HEADS * HEAD_DIM) ** -0.5),
        'ffn_norm': gain(ks[15], (DEPTH, D_MODEL)),
        'ffn_w_gate': nrm(ks[16], (DEPTH, D_MODEL, FFN_HIDDEN), D_MODEL ** -0.5),
        'ffn_w_up': nrm(ks[17], (DEPTH, D_MODEL, FFN_HIDDEN), D_MODEL ** -0.5),
        'ffn_w_down': nrm(ks[18], (DEPTH, FFN_HIDDEN, D_MODEL), FFN_HIDDEN ** -0.5),
    }


def reference(x_prompt, x_sample, ab_norm, ab_w_in, gla_up_f, gla_bias_f, gla_up_b, gla_bias_b, gla_out_norm,
              ab_w_out, c_norm, c_w_in, c_q_norm, c_k_norm, c_w_out, ffn_norm, ffn_w_gate, ffn_w_up, ffn_w_down):
    y_prompt = trunk(x_prompt, ab_norm, ab_w_in, gla_up_f, gla_bias_f, gla_up_b, gla_bias_b, gla_out_norm, ab_w_out,
                     c_norm, c_w_in, c_q_norm, c_k_norm, c_w_out, ffn_norm, ffn_w_gate, ffn_w_up, ffn_w_down)
    y_sample = trunk(x_sample, ab_norm, ab_w_in, gla_up_f, gla_bias_f, gla_up_b, gla_bias_b, gla_out_norm, ab_w_out,
                     c_norm, c_w_in, c_q_norm, c_k_norm, c_w_out, ffn_norm, ffn_w_gate, ffn_w_up, ffn_w_down)
    return (y_prompt, y_sample)
```

```python
import functools

import numpy as np
import jax
import jax.numpy as jnp
from jax import lax
from jax.experimental import pallas as pl
from jax.experimental.pallas import tpu as pltpu

D_MODEL = 1024
GRID_W = 64
NORM_EPS = 1e-6
GLA_HEADS = 4
GLA_DK = 64
GLA_DV = 128
GLA_RANK = 16
GLA_GATE_NORM = 16.0
GLA_CHUNK = 64
FNET_GROUPS = 4
FNET_GW = 128
ATTN_HEADS = 8
KV_HEADS = 2
HEAD_DIM = 128
ROPE_THETA = 10000.0
FFN_HIDDEN = -(-8 * D_MODEL // (3 * 256)) * 256

GLA_QK = GLA_HEADS * GLA_DK
GLA_V = GLA_HEADS * GLA_DV
FNET_W = FNET_GROUPS * FNET_GW
ATTN_Q = ATTN_HEADS * HEAD_DIM
ATTN_KV = KV_HEADS * HEAD_DIM

SUBLANES = 8
VMEM_LIMIT = 56 * 1024 * 1024

BF16 = jnp.bfloat16
F32 = jnp.float32

_NT = (((1,), (1,)), ((), ()))
_TN = (((0,), (0,)), ((), ()))


def _params(*sem):
    return pltpu.CompilerParams(dimension_semantics=sem, vmem_limit_bytes=VMEM_LIMIT)


def _const_spec(shape):
    nd = len(shape)
    return pl.BlockSpec(shape, lambda *_: (0,) * nd, pipeline_mode=pl.Buffered(1))


def _dot(a, b):
    return jnp.dot(a, b, preferred_element_type=F32)


def _split_bf16(a):
    hi = a.astype(BF16)
    lo = (a - hi.astype(F32)).astype(BF16)
    return hi, lo


def _rms(x, gain):
    return x * lax.rsqrt(jnp.mean(x * x, axis=-1, keepdims=True) + NORM_EPS) * gain


def _ab_in_kernel(x_ref, ng_ref, wm_ref, wl_ref, wf_ref, up_ref, bias_ref,
                  q_ref, k_ref, v_ref, r_ref, gf_ref, gb_ref, f_ref):
    h = _rms(x_ref[...], ng_ref[...]).astype(BF16)
    y = _dot(h, wm_ref[...])
    q_ref[...] = (y[:, :GLA_QK] * (GLA_DK ** -0.5)).astype(BF16)
    k_ref[...] = y[:, GLA_QK:2 * GLA_QK].astype(BF16)
    v_ref[...] = y[:, 2 * GLA_QK:2 * GLA_QK + GLA_V].astype(BF16)
    r_ref[...] = y[:, 2 * GLA_QK + GLA_V:].astype(BF16)
    f_ref[...] = _dot(h, wf_ref[...]).astype(BF16)
    low = _dot(h, wl_ref[...])
    low_hi, low_lo = _split_bf16(low)
    up_hi, up_lo = _split_bf16(up_ref[...])
    z = _dot(low_hi, up_hi) + _dot(low_hi, up_lo) + _dot(low_lo, up_hi) + bias_ref[...]
    g = (jnp.minimum(z, 0.0) - jnp.log(1.0 + jnp.exp(-jnp.abs(z)))) * (1.0 / GLA_GATE_NORM)
    gf_ref[...] = g[:, :GLA_QK]
    gb_ref[...] = g[:, GLA_QK:]


def _ab_in(x, ng, wm, wl, wf, up, bias, tm):
    t = x.shape[0]
    row = lambda w: pl.BlockSpec((tm, w), lambda i: (i, 0))
    outs = [(GLA_QK, BF16), (GLA_QK, BF16), (GLA_V, BF16), (GLA_V, BF16),
            (GLA_QK, F32), (GLA_QK, F32), (FNET_W, BF16)]
    return pl.pallas_call(
        _ab_in_kernel,
        grid=(t // tm,),
        in_specs=[row(D_MODEL)] + [_const_spec(a.shape) for a in (ng, wm, wl, wf, up, bias)],
        out_specs=[row(w) for w, _ in outs],
        out_shape=[jax.ShapeDtypeStruct((t, w), d) for w, d in outs],
        compiler_params=_params("parallel"),
        name="ab_in",
    )(x, ng, wm, wl, wf, up, bias)


def _gla_chunk(q, k, v, g, state_ref, tri, keep, total_row):
    g_hi, g_lo = _split_bf16(g)
    bc = _dot(tri, g_hi) + _dot(tri, g_lo)
    e_tot = jnp.exp(bc[total_row:total_row + 1, :])
    q_in = q.astype(F32) * jnp.exp(bc)
    k_in = k.astype(F32) * jnp.exp(-bc)
    k_dec = (k_in * e_tot).astype(BF16)
    k_in = k_in.astype(BF16)
    lane = lax.broadcasted_iota(jnp.int32, (1, 2 * GLA_DK), 1)
    outs = []
    for h in range(GLA_HEADS):
        p, hh = divmod(h, 2)
        pair = slice(2 * GLA_DK * p, 2 * GLA_DK * (p + 1))
        own = (lane >= GLA_DK * hh) & (lane < GLA_DK * (hh + 1))
        qm = jnp.where(own, q_in[:, pair], 0.0).astype(BF16)
        att = lax.dot_general(qm, k_in[:, pair], _NT, preferred_element_type=F32)
        att = jnp.where(keep, att, 0.0).astype(BF16)
        vh = v[:, GLA_DV * h:GLA_DV * (h + 1)]
        st = state_ref[h]
        o = _dot(att, vh) + lax.dot_general(qm, st.astype(BF16), _NT, preferred_element_type=F32)
        outs.append(o)
        upd = lax.dot_general(vh, k_dec[:, pair], _TN, preferred_element_type=F32)
        state_ref[h] = st * e_tot[:, pair] + upd
    return jnp.concatenate(outs, axis=-1)


def _gla_kernel(qf_ref, kf_ref, vf_ref, gf_ref, qb_ref, kb_ref, vb_ref, gb_ref,
                of_ref, ob_ref, sf_ref, sb_ref, *, chunks):
    @pl.when(pl.program_id(1) == 0)
    def _():
        sf_ref[...] = jnp.zeros_like(sf_ref)
        sb_ref[...] = jnp.zeros_like(sb_ref)

    c = GLA_CHUNK
    ri = lax.broadcasted_iota(jnp.int32, (c, c), 0)
    ci = lax.broadcasted_iota(jnp.int32, (c, c), 1)
    lower = ri >= ci
    upper = ri <= ci
    tri_f = jnp.where(lower, 1.0, 0.0).astype(BF16)
    tri_b = jnp.where(upper, 1.0, 0.0).astype(BF16)
    for j in range(chunks):
        rows = pl.ds(j * c, c)
        of_ref[0, rows, :] = _gla_chunk(qf_ref[0, rows, :], kf_ref[0, rows, :], vf_ref[0, rows, :],
                                        gf_ref[0, rows, :], sf_ref, tri_f, lower, c - 1)
        rows = pl.ds((chunks - 1 - j) * c, c)
        ob_ref[0, rows, :] = _gla_chunk(qb_ref[0, rows, :], kb_ref[0, rows, :], vb_ref[0, rows, :],
                                        gb_ref[0, rows, :], sb_ref, tri_b, upper, 0)


def _gla(q, k, v, gf, gb, blk):
    b, s, _ = q.shape
    nb = s // blk
    fwd = lambda w: pl.BlockSpec((1, blk, w), lambda i, n: (i, n, 0))
    bwd = lambda w: pl.BlockSpec((1, blk, w), lambda i, n: (i, nb - 1 - n, 0))
    state = pltpu.VMEM((GLA_HEADS, GLA_DV, 2 * GLA_DK), F32)
    return pl.pallas_call(
        functools.partial(_gla_kernel, chunks=blk // GLA_CHUNK),
        grid=(b, nb),
        in_specs=[fwd(GLA_QK), fwd(GLA_QK), fwd(GLA_V), fwd(GLA_QK),
                  bwd(GLA_QK), bwd(GLA_QK), bwd(GLA_V), bwd(GLA_QK)],
        out_specs=[fwd(GLA_V), bwd(GLA_V)],
        out_shape=[jax.ShapeDtypeStruct((b, s, GLA_V), F32)] * 2,
        scratch_shapes=[state, state],
        compiler_params=_params("parallel", "arbitrary"),
        name="gla",
    )(q, k, v, gf, q, k, v, gb)


def _fnet_kernel(x_ref, ww_ref, k1c_ref, k1s_ref, k2c_ref, k2s_ref, twc_ref, tws_ref,
                 o_ref, z_ref, a_ref, *, nl, nm, groups):
    gw = FNET_GW
    cw = groups * gw
    rows_per = SUBLANES * nm
    for i in range(nl // SUBLANES):
        xs = x_ref[0, pl.ds(i * rows_per, rows_per), :]
        zr, zi = [], []
        for g in range(groups):
            zz = _dot(xs[:, g * gw:(g + 1) * gw], ww_ref[...])
            zr.append(zz[:, :gw])
            zi.append(zz[:, gw:])
        zcat = jnp.concatenate(zr + zi, axis=-1)
        z_ref[pl.ds(i * SUBLANES, SUBLANES)] = zcat.reshape(SUBLANES, nm // SUBLANES, SUBLANES, 2 * cw)
    for mb in range(nm // SUBLANES):
        z = z_ref[:, mb].reshape(nl * SUBLANES, 2 * cw)
        zr, zi = z[:, :cw], z[:, cw:]
        za = z.astype(BF16)
        zb = jnp.concatenate([zi, -zr], axis=-1).astype(BF16)
        a = _dot(k1c_ref[...], za) + _dot(k1s_ref[...], zb)
        ar, ai = a[:, :cw], a[:, cw:]
        tc = twc_ref[:, mb].reshape(nl * SUBLANES, gw)
        ts = tws_ref[:, mb].reshape(nl * SUBLANES, gw)
        tc = jnp.concatenate([tc] * groups, axis=-1)
        ts = jnp.concatenate([ts] * groups, axis=-1)
        pr = ar * tc + ai * ts
        pi = ai * tc - ar * ts
        a_ref[:, mb] = jnp.concatenate([pr, pi], axis=-1).reshape(nl, SUBLANES, 2 * cw)
    for ab in range(nl // SUBLANES):
        bb = a_ref[pl.ds(ab * SUBLANES, SUBLANES)].reshape(SUBLANES * nm, 2 * cw).astype(BF16)
        res = _dot(k2c_ref[...], bb[:, :cw]) + _dot(k2s_ref[...], bb[:, cw:])
        o_ref[0, :, ab] = res.reshape(nm, SUBLANES, cw)


def _dft_tables(s):
    nm = 64
    nl = s // nm
    assert nl * nm == s and nl % SUBLANES == 0
    eye = np.eye(SUBLANES)

    def dft(n):
        idx = np.arange(n)
        ang = 2.0 * np.pi * ((idx[:, None] * idx[None, :]) % n) / n
        return np.cos(ang), np.sin(ang)

    k1c, k1s = (np.kron(f, eye) for f in dft(nl))
    k2c, k2s = (np.einsum("nm,ij->nijm", f, eye).reshape(nm * SUBLANES, SUBLANES * nm)
                for f in dft(nm))
    a = np.arange(nl)[:, None]
    m = np.arange(nm)[None, :]
    ang = 2.0 * np.pi * ((a * m) % s) / s
    norm = 1.0 / np.sqrt(float(s) * FNET_GW)
    shape = (nl, nm // SUBLANES, SUBLANES, FNET_GW)
    twc = np.broadcast_to((np.cos(ang) * norm).reshape(nl, nm // SUBLANES, SUBLANES, 1), shape)
    tws = np.broadcast_to((np.sin(ang) * norm).reshape(nl, nm // SUBLANES, SUBLANES, 1), shape)
    w = np.arange(FNET_GW)
    angw = 2.0 * np.pi * ((w[:, None] * w[None, :]) % FNET_GW) / FNET_GW
    ww = np.concatenate([np.cos(angw), -np.sin(angw)], axis=1)
    bf = lambda v: jnp.asarray(v, dtype=F32).astype(BF16)
    return (nl, nm, bf(ww), bf(k1c), bf(k1s), bf(k2c), bf(k2s),
            jnp.asarray(twc, dtype=F32), jnp.asarray(tws, dtype=F32))


def _fnet(f, groups):
    b, s, _ = f.shape
    nl, nm, ww, k1c, k1s, k2c, k2s, twc, tws = _dft_tables(s)
    cw = groups * FNET_GW
    consts = (ww, k1c, k1s, k2c, k2s, twc, tws)
    out = pl.pallas_call(
        functools.partial(_fnet_kernel, nl=nl, nm=nm, groups=groups),
        grid=(b, FNET_W // cw),
        in_specs=[pl.BlockSpec((1, s, cw), lambda i, j: (i, 0, j))]
                 + [_const_spec(a.shape) for a in consts],
        out_specs=pl.BlockSpec((1, nm, nl // SUBLANES, SUBLANES, cw), lambda i, j: (i, 0, 0, 0, j)),
        out_shape=jax.ShapeDtypeStruct((b, nm, nl // SUBLANES, SUBLANES, FNET_W), F32),
        scratch_shapes=[pltpu.VMEM((nl, nm // SUBLANES, SUBLANES, 2 * cw), F32),
                        pltpu.VMEM((nl, nm // SUBLANES, SUBLANES, 2 * cw), F32)],
        compiler_params=_params("parallel", "parallel"),
        name="fnet",
    )(f, *consts)
    return out.reshape(b, s, FNET_W)


def _ab_out_kernel(of_ref, ob_ref, r_ref, fo_ref, x_ref, og_ref, wo_ref, wf_ref, y_ref):
    o = of_ref[...] + ob_ref[...]
    heads = [_rms(o[:, GLA_DV * h:GLA_DV * (h + 1)], og_ref[...]) for h in range(GLA_HEADS)]
    r = r_ref[...].astype(F32)
    o = jnp.concatenate(heads, axis=-1) * (r * jax.nn.sigmoid(r))
    y = _dot(o.astype(BF16), wo_ref[...]) + _dot(fo_ref[...].astype(BF16), wf_ref[...])
    y_ref[...] = x_ref[...] + y


def _ab_out(o_f, o_b, r, fo, x, og, wo, wf, tm):
    t = x.shape[0]
    row = lambda w: pl.BlockSpec((tm, w), lambda i: (i, 0))
    return pl.pallas_call(
        _ab_out_kernel,
        grid=(t // tm,),
        in_specs=[row(GLA_V), row(GLA_V), row(GLA_V), row(FNET_W), row(D_MODEL)]
                 + [_const_spec(a.shape) for a in (og, wo, wf)],
        out_specs=row(D_MODEL),
        out_shape=jax.ShapeDtypeStruct((t, D_MODEL), F32),
        compiler_params=_params("parallel"),
        name="ab_out",
    )(o_f, o_b, r, fo, x, og, wo, wf)


def _ffn_kernel(x_ref, ng_ref, wg_ref, wu_ref, wd_ref, y_ref):
    x = x_ref[...]
    h = _rms(x, ng_ref[...]).astype(BF16)
    g = _dot(h, wg_ref[...])
    u = _dot(h, wu_ref[...])
    a = (g * jax.nn.sigmoid(g) * u).astype(BF16)
    y_ref[...] = x + _dot(a, wd_ref[...])


def _ffn(x, ng, wg, wu, wd, tm):
    t = x.shape[0]
    row = pl.BlockSpec((tm, D_MODEL), lambda i: (i, 0))
    return pl.pallas_call(
        _ffn_kernel,
        grid=(t // tm,),
        in_specs=[row] + [_const_spec(a.shape) for a in (ng, wg, wu, wd)],
        out_specs=row,
        out_shape=jax.ShapeDtypeStruct((t, D_MODEL), F32),
        compiler_params=_params("parallel"),
        name="ffn",
    )(x, ng, wg, wu, wd)


def _rope(xh, cos, sin_a, sin_b):
    quarter = HEAD_DIM // 4
    return (xh * cos + pltpu.roll(xh, HEAD_DIM - quarter, 1) * sin_a
            + pltpu.roll(xh, quarter, 1) * sin_b)


def _c_in_kernel(x_ref, ng_ref, w_ref, qg_ref, kg_ref, cos_ref, sa_ref, sb_ref,
                 q_ref, k_ref, v_ref):
    h = _rms(x_ref[...], ng_ref[...]).astype(BF16)
    y = _dot(h, w_ref[...])
    cos, sa, sb = cos_ref[...], sa_ref[...], sb_ref[...]
    for i in range(ATTN_HEADS):
        cols = slice(HEAD_DIM * i, HEAD_DIM * (i + 1))
        q_ref[:, cols] = _rope(_rms(y[:, cols], qg_ref[...]), cos, sa, sb).astype(BF16)
    for i in range(KV_HEADS):
        cols = slice(HEAD_DIM * i, HEAD_DIM * (i + 1))
        kh = y[:, ATTN_Q + HEAD_DIM * i:ATTN_Q + HEAD_DIM * (i + 1)]
        k_ref[:, cols] = _rope(_rms(kh, kg_ref[...]), cos, sa, sb).astype(BF16)
    v_ref[...] = y[:, ATTN_Q + ATTN_KV:].astype(BF16)


def _rope_tables(s):
    quarter = HEAD_DIM // 4
    pos = jnp.arange(s)
    row = (pos // GRID_W).astype(F32)[:, None]
    col = (pos % GRID_W).astype(F32)[:, None]
    half = HEAD_DIM // 2
    inv = ROPE_THETA ** (-jnp.arange(0, half, 2, dtype=F32) / half)
    ar, ac = row * inv[None, :], col * inv[None, :]
    ang = jnp.concatenate([ar, ar, ac, ac], axis=-1)
    lane = jnp.arange(HEAD_DIM)[None, :]
    first = (lane % half) < quarter
    cos = jnp.cos(ang)
    sin = jnp.sin(ang)
    return cos, jnp.where(first, -sin, 0.0), jnp.where(first, 0.0, sin)


def _c_in(x, ng, w, qg, kg, tables, s, tm):
    t = x.shape[0]
    per_seq = s // tm
    row = lambda wd: pl.BlockSpec((tm, wd), lambda i: (i, 0))
    tab = pl.BlockSpec((tm, HEAD_DIM), lambda i: (i % per_seq, 0))
    outs = [(ATTN_Q, BF16), (ATTN_KV, BF16), (ATTN_KV, BF16)]
    return pl.pallas_call(
        _c_in_kernel,
        grid=(t // tm,),
        in_specs=[row(D_MODEL)] + [_const_spec(a.shape) for a in (ng, w, qg, kg)] + [tab] * 3,
        out_specs=[row(wd) for wd, _ in outs],
        out_shape=[jax.ShapeDtypeStruct((t, wd), d) for wd, d in outs],
        compiler_params=_params("parallel"),
        name="c_in",
    )(x, ng, w, qg, kg, *tables)


def _attn_kernel(q_ref, k_ref, v_ref, o_ref):
    group = ATTN_HEADS // KV_HEADS
    k = k_ref[0]
    v = v_ref[0]
    scale = HEAD_DIM ** -0.5
    for g in range(group):
        cols = slice(HEAD_DIM * g, HEAD_DIM * (g + 1))
        sc = lax.dot_general(q_ref[0, :, cols], k, _NT, preferred_element_type=F32) * scale
        m = jnp.max(sc, axis=-1, keepdims=True)
        p = jnp.exp(sc - m)
        denom = jnp.sum(p, axis=-1, keepdims=True)
        pr = (p / denom).astype(BF16)
        o_ref[0, :, cols] = _dot(pr, v).astype(o_ref.dtype)


def _attn(q, k, v, tq):
    b, s, _ = q.shape
    gw = ATTN_Q // KV_HEADS
    return pl.pallas_call(
        _attn_kernel,
        grid=(b, KV_HEADS, s // tq),
        in_specs=[pl.BlockSpec((1, tq, gw), lambda i, j, n: (i, n, j)),
                  pl.BlockSpec((1, s, HEAD_DIM), lambda i, j, n: (i, 0, j)),
                  pl.BlockSpec((1, s, HEAD_DIM), lambda i, j, n: (i, 0, j))],
        out_specs=pl.BlockSpec((1, tq, gw), lambda i, j, n: (i, n, j)),
        out_shape=jax.ShapeDtypeStruct((b, s, ATTN_Q), BF16),
        compiler_params=_params("parallel", "parallel", "arbitrary"),
        name="attn",
    )(q, k, v)


def _c_out_kernel(o_ref, x_ref, w_ref, y_ref):
    y_ref[...] = x_ref[...] + _dot(o_ref[...], w_ref[...])


def _c_out(o, x, w, tm):
    t = x.shape[0]
    row = lambda wd: pl.BlockSpec((tm, wd), lambda i: (i, 0))
    return pl.pallas_call(
        _c_out_kernel,
        grid=(t // tm,),
        in_specs=[row(ATTN_Q), row(D_MODEL), _const_spec(w.shape)],
        out_specs=row(D_MODEL),
        out_shape=jax.ShapeDtypeStruct((t, D_MODEL), F32),
        compiler_params=_params("parallel"),
        name="c_out",
    )(o, x, w)


def _prep_weights(ab_norm, ab_w_in, gla_up_f, gla_bias_f, gla_up_b, gla_bias_b, gla_out_norm,
                  ab_w_out, c_norm, c_w_in, c_q_norm, c_k_norm, c_w_out, ffn_norm,
                  ffn_w_gate, ffn_w_up, ffn_w_down):
    n_main = 2 * GLA_QK + 2 * GLA_V
    w_in = ab_w_in[0]
    zeros = jnp.zeros((GLA_RANK, GLA_QK), F32)
    up = jnp.concatenate([jnp.concatenate([gla_up_f[0], zeros], axis=1),
                          jnp.concatenate([zeros, gla_up_b[0]], axis=1)], axis=0)
    return dict(
        ab_ng=ab_norm[0][None, :],
        ab_wm=w_in[:, :n_main].astype(BF16),
        ab_wl=w_in[:, n_main:n_main + 2 * GLA_RANK].astype(BF16),
        ab_wf=w_in[:, n_main + 2 * GLA_RANK:].astype(BF16),
        ab_up=up,
        ab_bias=jnp.concatenate([gla_bias_f[0], gla_bias_b[0]])[None, :],
        ab_og=gla_out_norm[0][None, :],
        ab_wo=ab_w_out[0][:GLA_V].astype(BF16),
        ab_wfo=ab_w_out[0][GLA_V:].astype(BF16),
        c_ng=c_norm[0][None, :],
        c_w=c_w_in[0].astype(BF16),
        c_qg=c_q_norm[0][None, :],
        c_kg=c_k_norm[0][None, :],
        c_wo=c_w_out[0].astype(BF16),
        ffn_ng=[ffn_norm[i][None, :] for i in range(2)],
        ffn_wg=[ffn_w_gate[i].astype(BF16) for i in range(2)],
        ffn_wu=[ffn_w_up[i].astype(BF16) for i in range(2)],
        ffn_wd=[ffn_w_down[i].astype(BF16) for i in range(2)],
    )


def _trunk(x, w, tm=512, tm_ffn=256, gla_blk=256, tq=256, fnet_groups=2):
    b, s, d = x.shape
    t = b * s
    xf = x.reshape(t, d)
    q, k, v, r, gf, gb, f = _ab_in(xf, w["ab_ng"], w["ab_wm"], w["ab_wl"], w["ab_wf"],
                                   w["ab_up"], w["ab_bias"], tm)
    seq = lambda a: a.reshape(b, s, a.shape[-1])
    o_f, o_b = _gla(seq(q), seq(k), seq(v), seq(gf), seq(gb), gla_blk)
    fo = _fnet(seq(f), fnet_groups)
    xf = _ab_out(o_f.reshape(t, GLA_V), o_b.reshape(t, GLA_V), r, fo.reshape(t, FNET_W), xf,
                 w["ab_og"], w["ab_wo"], w["ab_wfo"], tm)
    xf = _ffn(xf, w["ffn_ng"][0], w["ffn_wg"][0], w["ffn_wu"][0], w["ffn_wd"][0], tm_ffn)
    q, k, v = _c_in(xf, w["c_ng"], w["c_w"], w["c_qg"], w["c_kg"], _rope_tables(s), s, tm)
    o = _attn(seq(q), seq(k), seq(v), tq)
    xf = _c_out(o.reshape(t, ATTN_Q), xf, w["c_wo"], tm)
    xf = _ffn(xf, w["ffn_ng"][1], w["ffn_wg"][1], w["ffn_wu"][1], w["ffn_wd"][1], tm_ffn)
    return xf.reshape(b, s, d)


def kernel(x_prompt, x_sample, ab_norm, ab_w_in, gla_up_f, gla_bias_f, gla_up_b, gla_bias_b,
           gla_out_norm, ab_w_out, c_norm, c_w_in, c_q_norm, c_k_norm, c_w_out, ffn_norm,
           ffn_w_gate, ffn_w_up, ffn_w_down):
    w = _prep_weights(ab_norm, ab_w_in, gla_up_f, gla_bias_f, gla_up_b, gla_bias_b, gla_out_norm,
                      ab_w_out, c_norm, c_w_in, c_q_norm, c_k_norm, c_w_out, ffn_norm,
                      ffn_w_gate, ffn_w_up, ffn_w_down)
    return _trunk(x_prompt, w), _trunk(x_sample, w)
```

```python
import functools

import numpy as np
import jax
import jax.numpy as jnp
from jax import lax
from jax.experimental import pallas as pl
from jax.experimental.pallas import tpu as pltpu

D_MODEL = 1024
GRID_W = 64
NORM_EPS = 1e-6
GLA_HEADS = 4
GLA_DK = 64
GLA_DV = 128
GLA_RANK = 16
GLA_GATE_NORM = 16.0
GLA_CHUNK = 64
FNET_GROUPS = 4
FNET_GW = 128
ATTN_HEADS = 8
KV_HEADS = 2
HEAD_DIM = 128
ROPE_THETA = 10000.0
FFN_HIDDEN = -(-8 * D_MODEL // (3 * 256)) * 256

GLA_QK = GLA_HEADS * GLA_DK
GLA_V = GLA_HEADS * GLA_DV
FNET_W = FNET_GROUPS * FNET_GW
ATTN_Q = ATTN_HEADS * HEAD_DIM
ATTN_KV = KV_HEADS * HEAD_DIM

SUBLANES = 8
VMEM_LIMIT = 56 * 1024 * 1024

BF16 = jnp.bfloat16
F32 = jnp.float32

_NT = (((1,), (1,)), ((), ()))
_TN = (((0,), (0,)), ((), ()))


def _params(*sem):
    return pltpu.CompilerParams(dimension_semantics=sem, vmem_limit_bytes=VMEM_LIMIT)


def _const_spec(shape):
    nd = len(shape)
    return pl.BlockSpec(shape, lambda *_: (0,) * nd, pipeline_mode=pl.Buffered(1))


def _dot(a, b):
    return jnp.dot(a, b, preferred_element_type=F32)


def _split_bf16(a):
    hi = a.astype(BF16)
    lo = (a - hi.astype(F32)).astype(BF16)
    return hi, lo


def _rms(x, gain):
    return x * lax.rsqrt(jnp.mean(x * x, axis=-1, keepdims=True) + NORM_EPS) * gain


def _ab_in_kernel(x_ref, ng_ref, wm_ref, wl_ref, wf_ref, up_ref, bias_ref,
                  q_ref, k_ref, v_ref, r_ref, gf_ref, gb_ref, f_ref):
    h = _rms(x_ref[...], ng_ref[...]).astype(BF16)
    y = _dot(h, wm_ref[...])
    q_ref[...] = (y[:, :GLA_QK] * (GLA_DK ** -0.5)).astype(BF16)
    k_ref[...] = y[:, GLA_QK:2 * GLA_QK].astype(BF16)
    v_ref[...] = y[:, 2 * GLA_QK:2 * GLA_QK + GLA_V].astype(BF16)
    r_ref[...] = y[:, 2 * GLA_QK + GLA_V:].astype(BF16)
    f_ref[...] = _dot(h, wf_ref[...]).astype(BF16)
    low = _dot(h, wl_ref[...])
    low_hi, low_lo = _split_bf16(low)
    up_hi, up_lo = _split_bf16(up_ref[...])
    z = _dot(low_hi, up_hi) + _dot(low_hi, up_lo) + _dot(low_lo, up_hi) + bias_ref[...]
    g = (jnp.minimum(z, 0.0) - jnp.log(1.0 + jnp.exp(-jnp.abs(z)))) * (1.0 / GLA_GATE_NORM)
    gf_ref[...] = g[:, :GLA_QK]
    gb_ref[...] = g[:, GLA_QK:]


def _ab_in(x, ng, wm, wl, wf, up, bias, tm):
    t = x.shape[0]
    row = lambda w: pl.BlockSpec((tm, w), lambda i: (i, 0))
    outs = [(GLA_QK, BF16), (GLA_QK, BF16), (GLA_V, BF16), (GLA_V, BF16),
            (GLA_QK, F32), (GLA_QK, F32), (FNET_W, BF16)]
    return pl.pallas_call(
        _ab_in_kernel,
        grid=(t // tm,),
        in_specs=[row(D_MODEL)] + [_const_spec(a.shape) for a in (ng, wm, wl, wf, up, bias)],
        out_specs=[row(w) for w, _ in outs],
        out_shape=[jax.ShapeDtypeStruct((t, w), d) for w, d in outs],
        compiler_params=_params("parallel"),
        name="ab_in",
    )(x, ng, wm, wl, wf, up, bias)


def _gla_direction(q_ref, k_ref, v_ref, g_ref, o_ref, state_ref, *, chunks, reverse):
    c = GLA_CHUNK
    n = chunks * c
    ri = lax.broadcasted_iota(jnp.int32, (n, n), 0)
    ci = lax.broadcasted_iota(jnp.int32, (n, n), 1)
    causal = (ri <= ci) if reverse else (ri >= ci)
    tri = jnp.where(causal & ((ri // c) == (ci // c)), 1.0, 0.0).astype(BF16)
    g_hi, g_lo = _split_bf16(g_ref[0])
    bc = _dot(tri, g_hi) + _dot(tri, g_lo)
    q_in = q_ref[0].astype(F32) * jnp.exp(bc)
    k_in = k_ref[0].astype(F32) * jnp.exp(-bc)
    v = v_ref[0]
    r2 = lax.broadcasted_iota(jnp.int32, (2 * c, c), 0) % c
    c2 = lax.broadcasted_iota(jnp.int32, (2 * c, c), 1)
    keep = (r2 <= c2) if reverse else (r2 >= c2)
    first = lax.broadcasted_iota(jnp.int32, (1, 2 * GLA_DK), 1) < GLA_DK
    states = [state_ref[p] for p in range(GLA_HEADS // 2)]
    for j in (reversed(range(chunks)) if reverse else range(chunks)):
        rows = slice(j * c, (j + 1) * c)
        tot = j * c if reverse else (j + 1) * c - 1
        e_tot = jnp.exp(bc[tot:tot + 1, :])
        k_c = k_in[rows]
        k_dec = (k_c * e_tot).astype(BF16)
        k_c = k_c.astype(BF16)
        outs = []
        for p in range(GLA_HEADS // 2):
            pair = slice(2 * GLA_DK * p, 2 * GLA_DK * (p + 1))
            qp = q_in[rows, pair]
            qm = jnp.concatenate([jnp.where(first, qp, 0.0), jnp.where(first, 0.0, qp)], axis=0).astype(BF16)
            att = lax.dot_general(qm, k_c[:, pair], _NT, preferred_element_type=F32)
            att = jnp.where(keep, att, 0.0).astype(BF16)
            vp = v[rows, 2 * GLA_DV * p:2 * GLA_DV * (p + 1)]
            st = states[p]
            st_b = st.astype(BF16)
            for hh in range(2):
                o = _dot(att[hh * c:(hh + 1) * c], vp[:, hh * GLA_DV:(hh + 1) * GLA_DV])
                o += lax.dot_general(qm[hh * c:(hh + 1) * c], st_b[hh * GLA_DV:(hh + 1) * GLA_DV], _NT,
                                     preferred_element_type=F32)
                outs.append(o)
            upd = lax.dot_general(vp, k_dec[:, pair], _TN, preferred_element_type=F32)
            states[p] = st * e_tot[:, pair] + upd
        o_ref[0, rows, :] = jnp.concatenate(outs, axis=-1)
    for p in range(GLA_HEADS // 2):
        state_ref[p] = states[p]


def _gla_kernel(qf_ref, kf_ref, vf_ref, gf_ref, qb_ref, kb_ref, vb_ref, gb_ref,
                of_ref, ob_ref, sf_ref, sb_ref, *, chunks):
    @pl.when(pl.program_id(1) == 0)
    def _():
        sf_ref[...] = jnp.zeros_like(sf_ref)
        sb_ref[...] = jnp.zeros_like(sb_ref)

    _gla_direction(qf_ref, kf_ref, vf_ref, gf_ref, of_ref, sf_ref, chunks=chunks, reverse=False)
    _gla_direction(qb_ref, kb_ref, vb_ref, gb_ref, ob_ref, sb_ref, chunks=chunks, reverse=True)


def _gla(q, k, v, gf, gb, blk):
    b, s, _ = q.shape
    nb = s // blk
    fwd = lambda w: pl.BlockSpec((1, blk, w), lambda i, n: (i, n, 0))
    bwd = lambda w: pl.BlockSpec((1, blk, w), lambda i, n: (i, nb - 1 - n, 0))
    state = pltpu.VMEM((GLA_HEADS // 2, 2 * GLA_DV, 2 * GLA_DK), F32)
    return pl.pallas_call(
        functools.partial(_gla_kernel, chunks=blk // GLA_CHUNK),
        grid=(b, nb),
        in_specs=[fwd(GLA_QK), fwd(GLA_QK), fwd(GLA_V), fwd(GLA_QK),
                  bwd(GLA_QK), bwd(GLA_QK), bwd(GLA_V), bwd(GLA_QK)],
        out_specs=[fwd(GLA_V), bwd(GLA_V)],
        out_shape=[jax.ShapeDtypeStruct((b, s, GLA_V), F32)] * 2,
        scratch_shapes=[state, state],
        compiler_params=_params("parallel", "arbitrary"),
        name="gla",
    )(q, k, v, gf, q, k, v, gb)


def _fnet_kernel(x_ref, ww_ref, k1c_ref, k1s_ref, k2c_ref, k2s_ref, twc_ref, tws_ref,
                 o_ref, z_ref, a_ref, *, nl, nm, groups):
    gw = FNET_GW
    cw = groups * gw
    rows_per = SUBLANES * nm
    for i in range(nl // SUBLANES):
        xs = x_ref[0, pl.ds(i * rows_per, rows_per), :]
        zr, zi = [], []
        for g in range(groups):
            zz = _dot(xs[:, g * gw:(g + 1) * gw], ww_ref[...])
            zr.append(zz[:, :gw])
            zi.append(zz[:, gw:])
        zcat = jnp.concatenate(zr + zi, axis=-1)
        z_ref[pl.ds(i * SUBLANES, SUBLANES)] = zcat.reshape(SUBLANES, nm // SUBLANES, SUBLANES, 2 * cw)
    for mb in range(nm // SUBLANES):
        z = z_ref[:, mb].reshape(nl * SUBLANES, 2 * cw)
        zr, zi = z[:, :cw], z[:, cw:]
        za = z.astype(BF16)
        zb = jnp.concatenate([zi, -zr], axis=-1).astype(BF16)
        a = _dot(k1c_ref[...], za) + _dot(k1s_ref[...], zb)
        ar, ai = a[:, :cw], a[:, cw:]
        tc = twc_ref[:, mb].reshape(nl * SUBLANES, gw)
        ts = tws_ref[:, mb].reshape(nl * SUBLANES, gw)
        tc = jnp.concatenate([tc] * groups, axis=-1)
        ts = jnp.concatenate([ts] * groups, axis=-1)
        pr = ar * tc + ai * ts
        pi = ai * tc - ar * ts
        a_ref[:, mb] = jnp.concatenate([pr, pi], axis=-1).reshape(nl, SUBLANES, 2 * cw)
    for ab in range(nl // SUBLANES):
        bb = a_ref[pl.ds(ab * SUBLANES, SUBLANES)].reshape(SUBLANES * nm, 2 * cw).astype(BF16)
        res = _dot(k2c_ref[...], bb[:, :cw]) + _dot(k2s_ref[...], bb[:, cw:])
        o_ref[0, :, ab] = res.reshape(nm, SUBLANES, cw)


def _dft_tables(s):
    nm = 64
    nl = s // nm
    assert nl * nm == s and nl % SUBLANES == 0
    eye = np.eye(SUBLANES)

    def dft(n):
        idx = np.arange(n)
        ang = 2.0 * np.pi * ((idx[:, None] * idx[None, :]) % n) / n
        return np.cos(ang), np.sin(ang)

    k1c, k1s = (np.kron(f, eye) for f in dft(nl))
    k2c, k2s = (np.einsum("nm,ij->nijm", f, eye).reshape(nm * SUBLANES, SUBLANES * nm)
                for f in dft(nm))
    a = np.arange(nl)[:, None]
    m = np.arange(nm)[None, :]
    ang = 2.0 * np.pi * ((a * m) % s) / s
    norm = 1.0 / np.sqrt(float(s) * FNET_GW)
    shape = (nl, nm // SUBLANES, SUBLANES, FNET_GW)
    twc = np.broadcast_to((np.cos(ang) * norm).reshape(nl, nm // SUBLANES, SUBLANES, 1), shape)
    tws = np.broadcast_to((np.sin(ang) * norm).reshape(nl, nm // SUBLANES, SUBLANES, 1), shape)
    w = np.arange(FNET_GW)
    angw = 2.0 * np.pi * ((w[:, None] * w[None, :]) % FNET_GW) / FNET_GW
    ww = np.concatenate([np.cos(angw), -np.sin(angw)], axis=1)
    bf = lambda v: jnp.asarray(v, dtype=F32).astype(BF16)
    return (nl, nm, bf(ww), bf(k1c), bf(k1s), bf(k2c), bf(k2s),
            jnp.asarray(twc, dtype=F32), jnp.asarray(tws, dtype=F32))


def _fnet(f, groups):
    b, s, _ = f.shape
    nl, nm, ww, k1c, k1s, k2c, k2s, twc, tws = _dft_tables(s)
    cw = groups * FNET_GW
    consts = (ww, k1c, k1s, k2c, k2s, twc, tws)
    out = pl.pallas_call(
        functools.partial(_fnet_kernel, nl=nl, nm=nm, groups=groups),
        grid=(b, FNET_W // cw),
        in_specs=[pl.BlockSpec((1, s, cw), lambda i, j: (i, 0, j))]
                 + [_const_spec(a.shape) for a in consts],
        out_specs=pl.BlockSpec((1, nm, nl // SUBLANES, SUBLANES, cw), lambda i, j: (i, 0, 0, 0, j)),
        out_shape=jax.ShapeDtypeStruct((b, nm, nl // SUBLANES, SUBLANES, FNET_W), F32),
        scratch_shapes=[pltpu.VMEM((nl, nm // SUBLANES, SUBLANES, 2 * cw), F32),
                        pltpu.VMEM((nl, nm // SUBLANES, SUBLANES, 2 * cw), F32)],
        compiler_params=_params("parallel", "parallel"),
        name="fnet",
    )(f, *consts)
    return out.reshape(b, s, FNET_W)


def _ab_out_kernel(of_ref, ob_ref, r_ref, fo_ref, x_ref, og_ref, wo_ref, wf_ref, y_ref):
    o = of_ref[...] + ob_ref[...]
    heads = [_rms(o[:, GLA_DV * h:GLA_DV * (h + 1)], og_ref[...]) for h in range(GLA_HEADS)]
    r = r_ref[...].astype(F32)
    o = jnp.concatenate(heads, axis=-1) * (r * jax.nn.sigmoid(r))
    y = _dot(o.astype(BF16), wo_ref[...]) + _dot(fo_ref[...].astype(BF16), wf_ref[...])
    y_ref[...] = x_ref[...] + y


def _ab_out(o_f, o_b, r, fo, x, og, wo, wf, tm):
    t = x.shape[0]
    row = lambda w: pl.BlockSpec((tm, w), lambda i: (i, 0))
    return pl.pallas_call(
        _ab_out_kernel,
        grid=(t // tm,),
        in_specs=[row(GLA_V), row(GLA_V), row(GLA_V), row(FNET_W), row(D_MODEL)]
                 + [_const_spec(a.shape) for a in (og, wo, wf)],
        out_specs=row(D_MODEL),
        out_shape=jax.ShapeDtypeStruct((t, D_MODEL), F32),
        compiler_params=_params("parallel"),
        name="ab_out",
    )(o_f, o_b, r, fo, x, og, wo, wf)


def _ffn_kernel(x_ref, ng_ref, wg_ref, wu_ref, wd_ref, y_ref):
    x = x_ref[...]
    h = _rms(x, ng_ref[...]).astype(BF16)
    g = _dot(h, wg_ref[...])
    u = _dot(h, wu_ref[...])
    a = (g * jax.nn.sigmoid(g) * u).astype(BF16)
    y_ref[...] = x + _dot(a, wd_ref[...])


def _ffn(x, ng, wg, wu, wd, tm):
    t = x.shape[0]
    row = pl.BlockSpec((tm, D_MODEL), lambda i: (i, 0))
    return pl.pallas_call(
        _ffn_kernel,
        grid=(t // tm,),
        in_specs=[row] + [_const_spec(a.shape) for a in (ng, wg, wu, wd)],
        out_specs=row,
        out_shape=jax.ShapeDtypeStruct((t, D_MODEL), F32),
        compiler_params=_params("parallel"),
        name="ffn",
    )(x, ng, wg, wu, wd)


def _rope(xh, cos, sin_a, sin_b):
    quarter = HEAD_DIM // 4
    return (xh * cos + pltpu.roll(xh, HEAD_DIM - quarter, 1) * sin_a
            + pltpu.roll(xh, quarter, 1) * sin_b)


def _c_in_kernel(x_ref, ng_ref, w_ref, qg_ref, kg_ref, cos_ref, sa_ref, sb_ref,
                 q_ref, k_ref, v_ref):
    h = _rms(x_ref[...], ng_ref[...]).astype(BF16)
    y = _dot(h, w_ref[...])
    cos, sa, sb = cos_ref[...], sa_ref[...], sb_ref[...]
    q_scale = (HEAD_DIM ** -0.5) * float(np.log2(np.e))
    for i in range(ATTN_HEADS):
        cols = slice(HEAD_DIM * i, HEAD_DIM * (i + 1))
        qh = _rope(_rms(y[:, cols], qg_ref[...]), cos, sa, sb)
        q_ref[:, cols] = (qh * q_scale).astype(BF16)
    ones = jnp.ones((y.shape[0], HEAD_DIM), BF16)
    for i in range(KV_HEADS):
        cols = slice(HEAD_DIM * i, HEAD_DIM * (i + 1))
        kh = y[:, ATTN_Q + HEAD_DIM * i:ATTN_Q + HEAD_DIM * (i + 1)]
        k_ref[:, cols] = _rope(_rms(kh, kg_ref[...]), cos, sa, sb).astype(BF16)
        vh = y[:, ATTN_Q + ATTN_KV + HEAD_DIM * i:ATTN_Q + ATTN_KV + HEAD_DIM * (i + 1)]
        v_ref[:, 2 * HEAD_DIM * i:2 * HEAD_DIM * (i + 1)] = jnp.concatenate([vh.astype(BF16), ones], axis=-1)


def _rope_tables(s):
    quarter = HEAD_DIM // 4
    pos = jnp.arange(s)
    row = (pos // GRID_W).astype(F32)[:, None]
    col = (pos % GRID_W).astype(F32)[:, None]
    half = HEAD_DIM // 2
    inv = ROPE_THETA ** (-jnp.arange(0, half, 2, dtype=F32) / half)
    ar, ac = row * inv[None, :], col * inv[None, :]
    ang = jnp.concatenate([ar, ar, ac, ac], axis=-1)
    lane = jnp.arange(HEAD_DIM)[None, :]
    first = (lane % half) < quarter
    cos = jnp.cos(ang)
    sin = jnp.sin(ang)
    return cos, jnp.where(first, -sin, 0.0), jnp.where(first, 0.0, sin)


def _c_in(x, ng, w, qg, kg, tables, s, tm):
    t = x.shape[0]
    per_seq = s // tm
    row = lambda wd: pl.BlockSpec((tm, wd), lambda i: (i, 0))
    tab = pl.BlockSpec((tm, HEAD_DIM), lambda i: (i % per_seq, 0))
    outs = [(ATTN_Q, BF16), (ATTN_KV, BF16), (2 * ATTN_KV, BF16)]
    return pl.pallas_call(
        _c_in_kernel,
        grid=(t // tm,),
        in_specs=[row(D_MODEL)] + [_const_spec(a.shape) for a in (ng, w, qg, kg)] + [tab] * 3,
        out_specs=[row(wd) for wd, _ in outs],
        out_shape=[jax.ShapeDtypeStruct((t, wd), d) for wd, d in outs],
        compiler_params=_params("parallel"),
        name="c_in",
    )(x, ng, w, qg, kg, *tables)


def _attn_kernel(q_ref, k_ref, v_ref, o_ref, s_ref, *, tq):
    group = ATTN_HEADS // KV_HEADS
    nblk = q_ref.shape[1] // tq

    def rows_of(r):
        return pl.ds(pl.multiple_of(r * tq, tq), tq)

    def scores(r, g, buf):
        q = q_ref[0, rows_of(r), HEAD_DIM * g:HEAD_DIM * (g + 1)]
        s_ref[buf] = lax.dot_general(q, k_ref[0], _NT, preferred_element_type=F32)

    def finish(r, g, buf):
        sc = s_ref[buf]
        m = jnp.max(sc, axis=-1, keepdims=True)
        ov = _dot(jnp.exp2(sc - m).astype(BF16), v_ref[0])
        out = ov[:, :HEAD_DIM] / ov[:, HEAD_DIM:]
        o_ref[0, rows_of(r), HEAD_DIM * g:HEAD_DIM * (g + 1)] = out.astype(o_ref.dtype)

    scores(0, 0, 0)

    def body(r, carry):
        for g in range(group):
            if g + 1 < group:
                scores(r, g + 1, (g + 1) % 2)
            else:
                scores(jnp.minimum(r + 1, nblk - 1), 0, 0)
            finish(r, g, g % 2)
        return carry

    lax.fori_loop(0, nblk, body, 0)


def _attn(q, k, v, tq):
    b, s, _ = q.shape
    gw = ATTN_Q // KV_HEADS
    return pl.pallas_call(
        functools.partial(_attn_kernel, tq=tq),
        grid=(b, KV_HEADS),
        in_specs=[pl.BlockSpec((1, s, gw), lambda i, j: (i, 0, j)),
                  pl.BlockSpec((1, s, HEAD_DIM), lambda i, j: (i, 0, j)),
                  pl.BlockSpec((1, s, 2 * HEAD_DIM), lambda i, j: (i, 0, j))],
        out_specs=pl.BlockSpec((1, s, gw), lambda i, j: (i, 0, j)),
        out_shape=jax.ShapeDtypeStruct((b, s, ATTN_Q), BF16),
        scratch_shapes=[pltpu.VMEM((2, tq, s), F32)],
        compiler_params=_params("parallel", "parallel"),
        name="attn",
    )(q, k, v)


def _c_out_kernel(o_ref, x_ref, w_ref, y_ref):
    y_ref[...] = x_ref[...] + _dot(o_ref[...], w_ref[...])


def _c_out(o, x, w, tm):
    t = x.shape[0]
    row = lambda wd: pl.BlockSpec((tm, wd), lambda i: (i, 0))
    return pl.pallas_call(
        _c_out_kernel,
        grid=(t // tm,),
        in_specs=[row(ATTN_Q), row(D_MODEL), _const_spec(w.shape)],
        out_specs=row(D_MODEL),
        out_shape=jax.ShapeDtypeStruct((t, D_MODEL), F32),
        compiler_params=_params("parallel"),
        name="c_out",
    )(o, x, w)


def _prep_weights(ab_norm, ab_w_in, gla_up_f, gla_bias_f, gla_up_b, gla_bias_b, gla_out_norm,
                  ab_w_out, c_norm, c_w_in, c_q_norm, c_k_norm, c_w_out, ffn_norm,
                  ffn_w_gate, ffn_w_up, ffn_w_down):
    n_main = 2 * GLA_QK + 2 * GLA_V
    w_in = ab_w_in[0]
    zeros = jnp.zeros((GLA_RANK, GLA_QK), F32)
    up = jnp.concatenate([jnp.concatenate([gla_up_f[0], zeros], axis=1),
                          jnp.concatenate([zeros, gla_up_b[0]], axis=1)], axis=0)
    return dict(
        ab_ng=ab_norm[0][None, :],
        ab_wm=w_in[:, :n_main].astype(BF16),
        ab_wl=w_in[:, n_main:n_main + 2 * GLA_RANK].astype(BF16),
        ab_wf=w_in[:, n_main + 2 * GLA_RANK:].astype(BF16),
        ab_up=up,
        ab_bias=jnp.concatenate([gla_bias_f[0], gla_bias_b[0]])[None, :],
        ab_og=gla_out_norm[0][None, :],
        ab_wo=ab_w_out[0][:GLA_V].astype(BF16),
        ab_wfo=ab_w_out[0][GLA_V:].astype(BF16),
        c_ng=c_norm[0][None, :],
        c_w=c_w_in[0].astype(BF16),
        c_qg=c_q_norm[0][None, :],
        c_kg=c_k_norm[0][None, :],
        c_wo=c_w_out[0].astype(BF16),
        ffn_ng=[ffn_norm[i][None, :] for i in range(2)],
        ffn_wg=[ffn_w_gate[i].astype(BF16) for i in range(2)],
        ffn_wu=[ffn_w_up[i].astype(BF16) for i in range(2)],
        ffn_wd=[ffn_w_down[i].astype(BF16) for i in range(2)],
    )


def _trunk(x, w, tm=512, tm_ffn=256, gla_blk=256, tq=256, fnet_groups=2):
    b, s, d = x.shape
    t = b * s
    xf = x.reshape(t, d)
    q, k, v, r, gf, gb, f = _ab_in(xf, w["ab_ng"], w["ab_wm"], w["ab_wl"], w["ab_wf"],
                                   w["ab_up"], w["ab_bias"], tm)
    seq = lambda a: a.reshape(b, s, a.shape[-1])
    o_f, o_b = _gla(seq(q), seq(k), seq(v), seq(gf), seq(gb), gla_blk)
    fo = _fnet(seq(f), fnet_groups)
    xf = _ab_out(o_f.reshape(t, GLA_V), o_b.reshape(t, GLA_V), r, fo.reshape(t, FNET_W), xf,
                 w["ab_og"], w["ab_wo"], w["ab_wfo"], tm)
    xf = _ffn(xf, w["ffn_ng"][0], w["ffn_wg"][0], w["ffn_wu"][0], w["ffn_wd"][0], tm_ffn)
    q, k, v = _c_in(xf, w["c_ng"], w["c_w"], w["c_qg"], w["c_kg"], _rope_tables(s), s, tm)
    o = _attn(seq(q), seq(k), seq(v), tq)
    xf = _c_out(o.reshape(t, ATTN_Q), xf, w["c_wo"], tm)
    xf = _ffn(xf, w["ffn_ng"][1], w["ffn_wg"][1], w["ffn_wu"][1], w["ffn_wd"][1], tm_ffn)
    return xf.reshape(b, s, d)


def kernel(x_prompt, x_sample, ab_norm, ab_w_in, gla_up_f, gla_bias_f, gla_up_b, gla_bias_b,
           gla_out_norm, ab_w_out, c_norm, c_w_in, c_q_norm, c_k_norm, c_w_out, ffn_norm,
           ffn_w_gate, ffn_w_up, ffn_w_down):
    w = _prep_weights(ab_norm, ab_w_in, gla_up_f, gla_bias_f, gla_up_b, gla_bias_b, gla_out_norm,
                      ab_w_out, c_norm, c_w_in, c_q_norm, c_k_norm, c_w_out, ffn_norm,
                      ffn_w_gate, ffn_w_up, ffn_w_down)
    return _trunk(x_prompt, w), _trunk(x_sample, w)
```

```python
import functools

import numpy as np
import jax
import jax.numpy as jnp
from jax import lax
from jax.experimental import pallas as pl
from jax.experimental.pallas import tpu as pltpu

D_MODEL = 1024
GRID_W = 64
NORM_EPS = 1e-6
GLA_HEADS = 4
GLA_DK = 64
GLA_DV = 128
GLA_RANK = 16
GLA_GATE_NORM = 16.0
GLA_CHUNK = 64
FNET_GROUPS = 4
FNET_GW = 128
ATTN_HEADS = 8
KV_HEADS = 2
HEAD_DIM = 128
ROPE_THETA = 10000.0
FFN_HIDDEN = -(-8 * D_MODEL // (3 * 256)) * 256

GLA_QK = GLA_HEADS * GLA_DK
GLA_V = GLA_HEADS * GLA_DV
FNET_W = FNET_GROUPS * FNET_GW
ATTN_Q = ATTN_HEADS * HEAD_DIM
ATTN_KV = KV_HEADS * HEAD_DIM

SUBLANES = 8
VMEM_LIMIT = 56 * 1024 * 1024

BF16 = jnp.bfloat16
F32 = jnp.float32

_NT = (((1,), (1,)), ((), ()))
_TN = (((0,), (0,)), ((), ()))


def _params(*sem):
    return pltpu.CompilerParams(dimension_semantics=sem, vmem_limit_bytes=VMEM_LIMIT)


def _const_spec(shape):
    nd = len(shape)
    return pl.BlockSpec(shape, lambda *_: (0,) * nd, pipeline_mode=pl.Buffered(1))


def _dot(a, b):
    return jnp.dot(a, b, preferred_element_type=F32)


def _split_bf16(a):
    hi = a.astype(BF16)
    lo = (a - hi.astype(F32)).astype(BF16)
    return hi, lo


def _rms(x, gain):
    return x * lax.rsqrt(jnp.mean(x * x, axis=-1, keepdims=True) + NORM_EPS) * gain


def _ab_in_kernel(x_ref, ng_ref, wm_ref, wl_ref, wf_ref, up_ref, bias_ref,
                  q_ref, k_ref, v_ref, r_ref, gf_ref, gb_ref, f_ref):
    h = _rms(x_ref[...], ng_ref[...]).astype(BF16)
    low = _dot(h, wl_ref[...])
    y = _dot(h, wm_ref[...])
    q_ref[...] = (y[:, :GLA_QK] * (GLA_DK ** -0.5)).astype(BF16)
    k_ref[...] = y[:, GLA_QK:2 * GLA_QK].astype(BF16)
    v_ref[...] = y[:, 2 * GLA_QK:2 * GLA_QK + GLA_V].astype(BF16)
    r_ref[...] = y[:, 2 * GLA_QK + GLA_V:].astype(BF16)
    low_hi, low_lo = _split_bf16(low)
    up_hi, up_lo = _split_bf16(up_ref[...])
    z = _dot(low_hi, up_hi) + _dot(low_hi, up_lo) + _dot(low_lo, up_hi) + bias_ref[...]
    g = (jnp.minimum(z, 0.0) - jnp.log(1.0 + jnp.exp(-jnp.abs(z)))) * (1.0 / GLA_GATE_NORM)
    gf_ref[...] = g[:, :GLA_QK]
    gb_ref[...] = g[:, GLA_QK:]
    f_ref[...] = _dot(h, wf_ref[...]).astype(BF16)


def _ab_in(x, ng, wm, wl, wf, up, bias, tm):
    t = x.shape[0]
    row = lambda w: pl.BlockSpec((tm, w), lambda i: (i, 0))
    outs = [(GLA_QK, BF16), (GLA_QK, BF16), (GLA_V, BF16), (GLA_V, BF16),
            (GLA_QK, F32), (GLA_QK, F32), (FNET_W, BF16)]
    return pl.pallas_call(
        _ab_in_kernel,
        grid=(t // tm,),
        in_specs=[row(D_MODEL)] + [_const_spec(a.shape) for a in (ng, wm, wl, wf, up, bias)],
        out_specs=[row(w) for w, _ in outs],
        out_shape=[jax.ShapeDtypeStruct((t, w), d) for w, d in outs],
        compiler_params=_params("parallel"),
        name="ab_in",
    )(x, ng, wm, wl, wf, up, bias)


def _gla_direction(q_ref, k_ref, v_ref, g_ref, o_ref, state_ref, *, chunks, reverse):
    c = GLA_CHUNK
    n = chunks * c
    ri = lax.broadcasted_iota(jnp.int32, (n, n), 0)
    ci = lax.broadcasted_iota(jnp.int32, (n, n), 1)
    causal = (ri <= ci) if reverse else (ri >= ci)
    tri = jnp.where(causal & ((ri // c) == (ci // c)), 1.0, 0.0).astype(BF16)
    g_hi, g_lo = _split_bf16(g_ref[0])
    bc = _dot(tri, g_hi) + _dot(tri, g_lo)
    q_in = q_ref[0].astype(F32) * jnp.exp(bc)
    k_in = k_ref[0].astype(F32) * jnp.exp(-bc)
    v = v_ref[0]
    r2 = lax.broadcasted_iota(jnp.int32, (2 * c, c), 0) % c
    c2 = lax.broadcasted_iota(jnp.int32, (2 * c, c), 1)
    keep = (r2 <= c2) if reverse else (r2 >= c2)
    first = lax.broadcasted_iota(jnp.int32, (1, 2 * GLA_DK), 1) < GLA_DK
    states = [state_ref[p] for p in range(GLA_HEADS // 2)]
    for j in (reversed(range(chunks)) if reverse else range(chunks)):
        rows = slice(j * c, (j + 1) * c)
        tot = j * c if reverse else (j + 1) * c - 1
        e_tot = jnp.exp(bc[tot:tot + 1, :])
        k_c = k_in[rows]
        k_dec = (k_c * e_tot).astype(BF16)
        k_c = k_c.astype(BF16)
        outs = []
        for p in range(GLA_HEADS // 2):
            pair = slice(2 * GLA_DK * p, 2 * GLA_DK * (p + 1))
            qp = q_in[rows, pair]
            qm = jnp.concatenate([jnp.where(first, qp, 0.0), jnp.where(first, 0.0, qp)], axis=0).astype(BF16)
            att = lax.dot_general(qm, k_c[:, pair], _NT, preferred_element_type=F32)
            att = jnp.where(keep, att, 0.0).astype(BF16)
            vp = v[rows, 2 * GLA_DV * p:2 * GLA_DV * (p + 1)]
            st = states[p]
            st_b = st.astype(BF16)
            for hh in range(2):
                o = _dot(att[hh * c:(hh + 1) * c], vp[:, hh * GLA_DV:(hh + 1) * GLA_DV])
                o += lax.dot_general(qm[hh * c:(hh + 1) * c], st_b[hh * GLA_DV:(hh + 1) * GLA_DV], _NT,
                                     preferred_element_type=F32)
                outs.append(o)
            upd = lax.dot_general(vp, k_dec[:, pair], _TN, preferred_element_type=F32)
            states[p] = st * e_tot[:, pair] + upd
        o_ref[0, rows, :] = jnp.concatenate(outs, axis=-1)
    for p in range(GLA_HEADS // 2):
        state_ref[p] = states[p]


def _gla_kernel(qf_ref, kf_ref, vf_ref, gf_ref, qb_ref, kb_ref, vb_ref, gb_ref,
                of_ref, ob_ref, sf_ref, sb_ref, *, chunks):
    @pl.when(pl.program_id(1) == 0)
    def _():
        sf_ref[...] = jnp.zeros_like(sf_ref)
        sb_ref[...] = jnp.zeros_like(sb_ref)

    _gla_direction(qf_ref, kf_ref, vf_ref, gf_ref, of_ref, sf_ref, chunks=chunks, reverse=False)
    _gla_direction(qb_ref, kb_ref, vb_ref, gb_ref, ob_ref, sb_ref, chunks=chunks, reverse=True)


def _gla(q, k, v, gf, gb, blk):
    b, s, _ = q.shape
    nb = s // blk
    fwd = lambda w: pl.BlockSpec((1, blk, w), lambda i, n: (i, n, 0))
    bwd = lambda w: pl.BlockSpec((1, blk, w), lambda i, n: (i, nb - 1 - n, 0))
    state = pltpu.VMEM((GLA_HEADS // 2, 2 * GLA_DV, 2 * GLA_DK), F32)
    return pl.pallas_call(
        functools.partial(_gla_kernel, chunks=blk // GLA_CHUNK),
        grid=(b, nb),
        in_specs=[fwd(GLA_QK), fwd(GLA_QK), fwd(GLA_V), fwd(GLA_QK),
                  bwd(GLA_QK), bwd(GLA_QK), bwd(GLA_V), bwd(GLA_QK)],
        out_specs=[fwd(GLA_V), bwd(GLA_V)],
        out_shape=[jax.ShapeDtypeStruct((b, s, GLA_V), F32)] * 2,
        scratch_shapes=[state, state],
        compiler_params=_params("parallel", "arbitrary"),
        name="gla",
    )(q, k, v, gf, q, k, v, gb)


def _fnet_kernel(x_ref, ww_ref, k1c_ref, k1s_ref, k2c_ref, k2s_ref, twc_ref, tws_ref,
                 o_ref, z_ref, a_ref, *, nl, nm, groups):
    gw = FNET_GW
    cw = groups * gw
    rows_per = SUBLANES * nm
    for i in range(nl // SUBLANES):
        xs = x_ref[0, pl.ds(i * rows_per, rows_per), :]
        zr, zi = [], []
        for g in range(groups):
            zz = _dot(xs[:, g * gw:(g + 1) * gw], ww_ref[...])
            zr.append(zz[:, :gw])
            zi.append(zz[:, gw:])
        zcat = jnp.concatenate(zr + zi, axis=-1)
        z_ref[pl.ds(i * SUBLANES, SUBLANES)] = zcat.reshape(SUBLANES, nm // SUBLANES, SUBLANES, 2 * cw)
    for mb in range(nm // SUBLANES):
        z = z_ref[:, mb].reshape(nl * SUBLANES, 2 * cw)
        zr, zi = z[:, :cw], z[:, cw:]
        za = z.astype(BF16)
        zb = jnp.concatenate([zi, -zr], axis=-1).astype(BF16)
        a = _dot(k1c_ref[...], za) + _dot(k1s_ref[...], zb)
        ar, ai = a[:, :cw], a[:, cw:]
        tc = twc_ref[:, mb].reshape(nl * SUBLANES, gw)
        ts = tws_ref[:, mb].reshape(nl * SUBLANES, gw)
        tc = jnp.concatenate([tc] * groups, axis=-1)
        ts = jnp.concatenate([ts] * groups, axis=-1)
        pr = ar * tc + ai * ts
        pi = ai * tc - ar * ts
        a_ref[:, mb] = jnp.concatenate([pr, pi], axis=-1).reshape(nl, SUBLANES, 2 * cw)
    for ab in range(nl // SUBLANES):
        bb = a_ref[pl.ds(ab * SUBLANES, SUBLANES)].reshape(SUBLANES * nm, 2 * cw).astype(BF16)
        res = _dot(k2c_ref[...], bb[:, :cw]) + _dot(k2s_ref[...], bb[:, cw:])
        o_ref[0, :, ab] = res.reshape(nm, SUBLANES, cw)


def _dft_tables(s):
    nm = 64
    nl = s // nm
    assert nl * nm == s and nl % SUBLANES == 0
    eye = np.eye(SUBLANES)

    def dft(n):
        idx = np.arange(n)
        ang = 2.0 * np.pi * ((idx[:, None] * idx[None, :]) % n) / n
        return np.cos(ang), np.sin(ang)

    k1c, k1s = (np.kron(f, eye) for f in dft(nl))
    k2c, k2s = (np.einsum("nm,ij->nijm", f, eye).reshape(nm * SUBLANES, SUBLANES * nm)
                for f in dft(nm))
    a = np.arange(nl)[:, None]
    m = np.arange(nm)[None, :]
    ang = 2.0 * np.pi * ((a * m) % s) / s
    norm = 1.0 / np.sqrt(float(s) * FNET_GW)
    shape = (nl, nm // SUBLANES, SUBLANES, FNET_GW)
    twc = np.broadcast_to((np.cos(ang) * norm).reshape(nl, nm // SUBLANES, SUBLANES, 1), shape)
    tws = np.broadcast_to((np.sin(ang) * norm).reshape(nl, nm // SUBLANES, SUBLANES, 1), shape)
    w = np.arange(FNET_GW)
    angw = 2.0 * np.pi * ((w[:, None] * w[None, :]) % FNET_GW) / FNET_GW
    ww = np.concatenate([np.cos(angw), -np.sin(angw)], axis=1)
    bf = lambda v: jnp.asarray(v, dtype=F32).astype(BF16)
    return (nl, nm, bf(ww), bf(k1c), bf(k1s), bf(k2c), bf(k2s),
            jnp.asarray(twc, dtype=F32), jnp.asarray(tws, dtype=F32))


def _fnet(f, groups):
    b, s, _ = f.shape
    nl, nm, ww, k1c, k1s, k2c, k2s, twc, tws = _dft_tables(s)
    cw = groups * FNET_GW
    consts = (ww, k1c, k1s, k2c, k2s, twc, tws)
    out = pl.pallas_call(
        functools.partial(_fnet_kernel, nl=nl, nm=nm, groups=groups),
        grid=(b, FNET_W // cw),
        in_specs=[pl.BlockSpec((1, s, cw), lambda i, j: (i, 0, j))]
                 + [_const_spec(a.shape) for a in consts],
        out_specs=pl.BlockSpec((1, nm, nl // SUBLANES, SUBLANES, cw), lambda i, j: (i, 0, 0, 0, j)),
        out_shape=jax.ShapeDtypeStruct((b, nm, nl // SUBLANES, SUBLANES, FNET_W), F32),
        scratch_shapes=[pltpu.VMEM((nl, nm // SUBLANES, SUBLANES, 2 * cw), F32),
                        pltpu.VMEM((nl, nm // SUBLANES, SUBLANES, 2 * cw), F32)],
        compiler_params=_params("parallel", "parallel"),
        name="fnet",
    )(f, *consts)
    return out.reshape(b, s, FNET_W)


FFN_SPLIT = 6 * 256


def _swiglu(x, ng_ref, wg_ref, wu_ref, wd_ref):
    h = _rms(x, ng_ref[...]).astype(BF16)
    y = x
    for cols in (slice(0, FFN_SPLIT), slice(FFN_SPLIT, FFN_HIDDEN)):
        g = _dot(h, wg_ref[:, cols])
        u = _dot(h, wu_ref[:, cols])
        a = (g * jax.nn.sigmoid(g) * u).astype(BF16)
        y = y + _dot(a, wd_ref[cols, :])
    return y


def _ab_out_ffn_kernel(of_ref, ob_ref, r_ref, fo_ref, x_ref, og_ref, wo_ref, wf_ref,
                       ng_ref, wg_ref, wu_ref, wd_ref, y_ref):
    o = of_ref[...] + ob_ref[...]
    heads = [_rms(o[:, GLA_DV * h:GLA_DV * (h + 1)], og_ref[...]) for h in range(GLA_HEADS)]
    r = r_ref[...].astype(F32)
    o = jnp.concatenate(heads, axis=-1) * (r * jax.nn.sigmoid(r))
    mixed = _dot(o.astype(BF16), wo_ref[...]) + _dot(fo_ref[...].astype(BF16), wf_ref[...])
    y_ref[...] = _swiglu(x_ref[...] + mixed, ng_ref, wg_ref, wu_ref, wd_ref)


def _c_out_ffn_kernel(o_ref, x_ref, w_ref, ng_ref, wg_ref, wu_ref, wd_ref, y_ref):
    x = x_ref[...] + _dot(o_ref[...], w_ref[...])
    y_ref[...] = _swiglu(x, ng_ref, wg_ref, wu_ref, wd_ref)


def _rows_call(body, name, row_args, const_args, tm):
    t = row_args[0].shape[0]
    row = lambda a: pl.BlockSpec((tm, a.shape[1]), lambda i: (i, 0))
    return pl.pallas_call(
        body,
        grid=(t // tm,),
        in_specs=[row(a) for a in row_args] + [_const_spec(a.shape) for a in const_args],
        out_specs=pl.BlockSpec((tm, D_MODEL), lambda i: (i, 0)),
        out_shape=jax.ShapeDtypeStruct((t, D_MODEL), F32),
        compiler_params=_params("parallel"),
        name=name,
    )(*row_args, *const_args)


_QUARTER = HEAD_DIM // 4
_HEAD_PERM = np.concatenate([np.arange(0, _QUARTER), np.arange(2 * _QUARTER, 3 * _QUARTER),
                             np.arange(_QUARTER, 2 * _QUARTER), np.arange(3 * _QUARTER, 4 * _QUARTER)])


def _qk_norm_rope(y2, ones_bd, cos, sin):
    sq = (y2 * y2).astype(BF16)
    ms = _dot(sq, ones_bd) * (1.0 / HEAD_DIM)
    inv = lax.rsqrt(ms + NORM_EPS)
    halves = []
    for j in range(2):
        yh = y2[:, HEAD_DIM * j:HEAD_DIM * (j + 1)]
        halves.append(yh * cos + pltpu.roll(yh, HEAD_DIM // 2, 1) * sin)
    return jnp.concatenate(halves, axis=-1) * inv


def _c_in_kernel(x_ref, ng_ref, w_ref, bd_ref, cq_ref, sq_ref, ck_ref, sk_ref, q_ref, k_ref, v_ref):
    h = _rms(x_ref[...], ng_ref[...]).astype(BF16)
    y = _dot(h, w_ref[...])
    ones_bd = bd_ref[...]
    cq, sq, ck, sk = cq_ref[...], sq_ref[...], ck_ref[...], sk_ref[...]
    for i in range(0, ATTN_HEADS, 2):
        cols = slice(HEAD_DIM * i, HEAD_DIM * (i + 2))
        q_ref[:, cols] = _qk_norm_rope(y[:, cols], ones_bd, cq, sq).astype(BF16)
    k_ref[...] = _qk_norm_rope(y[:, ATTN_Q:ATTN_Q + ATTN_KV], ones_bd, ck, sk).astype(BF16)
    ones = jnp.ones((y.shape[0], HEAD_DIM), BF16)
    for i in range(KV_HEADS):
        vh = y[:, ATTN_Q + ATTN_KV + HEAD_DIM * i:ATTN_Q + ATTN_KV + HEAD_DIM * (i + 1)]
        v_ref[:, 2 * HEAD_DIM * i:2 * HEAD_DIM * (i + 1)] = jnp.concatenate([vh.astype(BF16), ones], axis=-1)


def _rope_tables(s, qg, kg):
    pos = jnp.arange(s)
    row = (pos // GRID_W).astype(F32)[:, None]
    col = (pos % GRID_W).astype(F32)[:, None]
    half = HEAD_DIM // 2
    inv = ROPE_THETA ** (-jnp.arange(0, half, 2, dtype=F32) / half)
    ar, ac = row * inv[None, :], col * inv[None, :]
    ang = jnp.concatenate([ar, ac, ar, ac], axis=-1)
    cos = jnp.cos(ang)
    sin = jnp.sin(ang) * jnp.where(jnp.arange(HEAD_DIM) < half, -1.0, 1.0)[None, :]
    q_scale = (HEAD_DIM ** -0.5) * float(np.log2(np.e))
    out = []
    for gain, scale in ((qg, q_scale), (kg, 1.0)):
        gp = gain[_HEAD_PERM] * scale
        out += [cos * gp[None, :], sin * jnp.roll(gp, half)[None, :]]
    return out


def _c_in(x, ng, w, tables, s, tm):
    t = x.shape[0]
    per_seq = s // tm
    row = lambda wd: pl.BlockSpec((tm, wd), lambda i: (i, 0))
    tab = pl.BlockSpec((tm, HEAD_DIM), lambda i: (i % per_seq, 0))
    outs = [(ATTN_Q, BF16), (ATTN_KV, BF16), (2 * ATTN_KV, BF16)]
    ones_bd = jnp.asarray(np.kron(np.eye(2), np.ones((HEAD_DIM, HEAD_DIM))), dtype=BF16)
    return pl.pallas_call(
        _c_in_kernel,
        grid=(t // tm,),
        in_specs=[row(D_MODEL)] + [_const_spec(a.shape) for a in (ng, w, ones_bd)] + [tab] * 4,
        out_specs=[row(wd) for wd, _ in outs],
        out_shape=[jax.ShapeDtypeStruct((t, wd), d) for wd, d in outs],
        compiler_params=_params("parallel"),
        name="c_in",
    )(x, ng, w, ones_bd, *tables)


def _attn_kernel(q_ref, k_ref, v_ref, o_ref, s_ref, *, tq):
    group = ATTN_HEADS // KV_HEADS
    nblk = q_ref.shape[1] // tq

    def rows_of(r):
        return pl.ds(pl.multiple_of(r * tq, tq), tq)

    def scores(r, g, buf):
        q = q_ref[0, rows_of(r), HEAD_DIM * g:HEAD_DIM * (g + 1)]
        s_ref[buf] = lax.dot_general(q, k_ref[0], _NT, preferred_element_type=F32)

    def finish(r, g, buf):
        sc = s_ref[buf]
        m = jnp.max(sc, axis=-1, keepdims=True)
        ov = _dot(jnp.exp2(sc - m).astype(BF16), v_ref[0])
        out = ov[:, :HEAD_DIM] / ov[:, HEAD_DIM:]
        o_ref[0, rows_of(r), HEAD_DIM * g:HEAD_DIM * (g + 1)] = out.astype(o_ref.dtype)

    scores(0, 0, 0)

    def body(r, carry):
        for g in range(group):
            if g + 1 < group:
                scores(r, g + 1, (g + 1) % 2)
            else:
                scores(jnp.minimum(r + 1, nblk - 1), 0, 0)
            finish(r, g, g % 2)
        return carry

    lax.fori_loop(0, nblk, body, 0)


def _attn(q, k, v, tq):
    b, s, _ = q.shape
    gw = ATTN_Q // KV_HEADS
    return pl.pallas_call(
        functools.partial(_attn_kernel, tq=tq),
        grid=(b, KV_HEADS),
        in_specs=[pl.BlockSpec((1, s, gw), lambda i, j: (i, 0, j)),
                  pl.BlockSpec((1, s, HEAD_DIM), lambda i, j: (i, 0, j)),
                  pl.BlockSpec((1, s, 2 * HEAD_DIM), lambda i, j: (i, 0, j))],
        out_specs=pl.BlockSpec((1, s, gw), lambda i, j: (i, 0, j)),
        out_shape=jax.ShapeDtypeStruct((b, s, ATTN_Q), BF16),
        scratch_shapes=[pltpu.VMEM((2, tq, s), F32)],
        compiler_params=_params("parallel", "parallel"),
        name="attn",
    )(q, k, v)


def _prep_weights(ab_norm, ab_w_in, gla_up_f, gla_bias_f, gla_up_b, gla_bias_b, gla_out_norm,
                  ab_w_out, c_norm, c_w_in, c_q_norm, c_k_norm, c_w_out, ffn_norm,
                  ffn_w_gate, ffn_w_up, ffn_w_down):
    n_main = 2 * GLA_QK + 2 * GLA_V
    w_in = ab_w_in[0]
    zeros = jnp.zeros((GLA_RANK, GLA_QK), F32)
    up = jnp.concatenate([jnp.concatenate([gla_up_f[0], zeros], axis=1),
                          jnp.concatenate([zeros, gla_up_b[0]], axis=1)], axis=0)
    qk_heads = ATTN_HEADS + KV_HEADS
    cols = np.concatenate([(HEAD_DIM * np.arange(qk_heads)[:, None] + _HEAD_PERM[None, :]).reshape(-1),
                           np.arange(HEAD_DIM * qk_heads, HEAD_DIM * (qk_heads + KV_HEADS))])
    return dict(
        ab_ng=ab_norm[0][None, :],
        ab_wm=w_in[:, :n_main].astype(BF16),
        ab_wl=w_in[:, n_main:n_main + 2 * GLA_RANK].astype(BF16),
        ab_wf=w_in[:, n_main + 2 * GLA_RANK:].astype(BF16),
        ab_up=up,
        ab_bias=jnp.concatenate([gla_bias_f[0], gla_bias_b[0]])[None, :],
        ab_og=gla_out_norm[0][None, :],
        ab_wo=ab_w_out[0][:GLA_V].astype(BF16),
        ab_wfo=ab_w_out[0][GLA_V:].astype(BF16),
        c_ng=c_norm[0][None, :],
        c_w=c_w_in[0][:, cols].astype(BF16),
        c_qg=c_q_norm[0],
        c_kg=c_k_norm[0],
        c_wo=c_w_out[0].astype(BF16),
        ffn_ng=[ffn_norm[i][None, :] for i in range(2)],
        ffn_wg=[ffn_w_gate[i].astype(BF16) for i in range(2)],
        ffn_wu=[ffn_w_up[i].astype(BF16) for i in range(2)],
        ffn_wd=[ffn_w_down[i].astype(BF16) for i in range(2)],
    )


def _trunk(x, w, tm=512, tm_ffn=512, gla_blk=256, tq=256, fnet_groups=2):
    b, s, d = x.shape
    t = b * s
    xf = x.reshape(t, d)
    q, k, v, r, gf, gb, f = _ab_in(xf, w["ab_ng"], w["ab_wm"], w["ab_wl"], w["ab_wf"],
                                   w["ab_up"], w["ab_bias"], tm)
    seq = lambda a: a.reshape(b, s, a.shape[-1])
    o_f, o_b = _gla(seq(q), seq(k), seq(v), seq(gf), seq(gb), gla_blk)
    fo = _fnet(seq(f), fnet_groups)
    ffn = lambda i: (w["ffn_ng"][i], w["ffn_wg"][i], w["ffn_wu"][i], w["ffn_wd"][i])
    xf = _rows_call(_ab_out_ffn_kernel, "ab_out_ffn",
                    (o_f.reshape(t, GLA_V), o_b.reshape(t, GLA_V), r, fo.reshape(t, FNET_W), xf),
                    (w["ab_og"], w["ab_wo"], w["ab_wfo"]) + ffn(0), tm_ffn)
    q, k, v = _c_in(xf, w["c_ng"], w["c_w"], _rope_tables(s, w["c_qg"], w["c_kg"]), s, tm)
    o = _attn(seq(q), seq(k), seq(v), tq)
    xf = _rows_call(_c_out_ffn_kernel, "c_out_ffn", (o.reshape(t, ATTN_Q), xf), (w["c_wo"],) + ffn(1), tm_ffn)
    return xf.reshape(b, s, d)


def kernel(x_prompt, x_sample, ab_norm, ab_w_in, gla_up_f, gla_bias_f, gla_up_b, gla_bias_b,
           gla_out_norm, ab_w_out, c_norm, c_w_in, c_q_norm, c_k_norm, c_w_out, ffn_norm,
           ffn_w_gate, ffn_w_up, ffn_w_down):
    w = _prep_weights(ab_norm, ab_w_in, gla_up_f, gla_bias_f, gla_up_b, gla_bias_b, gla_out_norm,
                      ab_w_out, c_norm, c_w_in, c_q_norm, c_k_norm, c_w_out, ffn_norm,
                      ffn_w_gate, ffn_w_up, ffn_w_down)
    return _trunk(x_prompt, w), _trunk(x_sample, w)
```

```python
import functools

import numpy as np
import jax
import jax.numpy as jnp
from jax import lax
from jax.experimental import pallas as pl
from jax.experimental.pallas import tpu as pltpu

D_MODEL = 1024
GRID_W = 64
NORM_EPS = 1e-6
GLA_HEADS = 4
GLA_DK = 64
GLA_DV = 128
GLA_RANK = 16
GLA_GATE_NORM = 16.0
GLA_CHUNK = 64
FNET_GROUPS = 4
FNET_GW = 128
ATTN_HEADS = 8
KV_HEADS = 2
HEAD_DIM = 128
ROPE_THETA = 10000.0
FFN_HIDDEN = -(-8 * D_MODEL // (3 * 256)) * 256

GLA_QK = GLA_HEADS * GLA_DK
GLA_V = GLA_HEADS * GLA_DV
FNET_W = FNET_GROUPS * FNET_GW
ATTN_Q = ATTN_HEADS * HEAD_DIM
ATTN_KV = KV_HEADS * HEAD_DIM

SUBLANES = 8
VMEM_LIMIT = 56 * 1024 * 1024

BF16 = jnp.bfloat16
F32 = jnp.float32

_NT = (((1,), (1,)), ((), ()))
_TN = (((0,), (0,)), ((), ()))


def _params(*sem):
    return pltpu.CompilerParams(dimension_semantics=sem, vmem_limit_bytes=VMEM_LIMIT)


def _const_spec(shape):
    nd = len(shape)
    return pl.BlockSpec(shape, lambda *_: (0,) * nd, pipeline_mode=pl.Buffered(1))


def _dot(a, b):
    return jnp.dot(a, b, preferred_element_type=F32)


def _split_bf16(a):
    hi = a.astype(BF16)
    lo = (a - hi.astype(F32)).astype(BF16)
    return hi, lo


def _rms(x, gain):
    return x * lax.rsqrt(jnp.mean(x * x, axis=-1, keepdims=True) + NORM_EPS) * gain


def _ab_in_kernel(x_ref, ng_ref, wm_ref, wl_ref, wf_ref, up_ref, bias_ref,
                  q_ref, k_ref, v_ref, r_ref, gf_ref, gb_ref, f_ref):
    h = _rms(x_ref[...], ng_ref[...]).astype(BF16)
    low = _dot(h, wl_ref[...])
    y = _dot(h, wm_ref[...])
    q_ref[...] = (y[:, :GLA_QK] * (GLA_DK ** -0.5)).astype(BF16)
    k_ref[...] = y[:, GLA_QK:2 * GLA_QK].astype(BF16)
    v_ref[...] = y[:, 2 * GLA_QK:2 * GLA_QK + GLA_V].astype(BF16)
    r_ref[...] = y[:, 2 * GLA_QK + GLA_V:].astype(BF16)
    low_hi, low_lo = _split_bf16(low)
    up_hi, up_lo = _split_bf16(up_ref[...])
    z = _dot(low_hi, up_hi) + _dot(low_hi, up_lo) + _dot(low_lo, up_hi) + bias_ref[...]
    g = (jnp.minimum(z, 0.0) - jnp.log(1.0 + jnp.exp(-jnp.abs(z)))) * (1.0 / GLA_GATE_NORM)
    gf_ref[...] = g[:, :GLA_QK]
    gb_ref[...] = g[:, GLA_QK:]
    f_ref[...] = _dot(h, wf_ref[...]).astype(BF16)


def _ab_in(x, ng, wm, wl, wf, up, bias, tm):
    t = x.shape[0]
    row = lambda w: pl.BlockSpec((tm, w), lambda i: (i, 0))
    outs = [(GLA_QK, BF16), (GLA_QK, BF16), (GLA_V, BF16), (GLA_V, BF16),
            (GLA_QK, F32), (GLA_QK, F32), (FNET_W, BF16)]
    return pl.pallas_call(
        _ab_in_kernel,
        grid=(t // tm,),
        in_specs=[row(D_MODEL)] + [_const_spec(a.shape) for a in (ng, wm, wl, wf, up, bias)],
        out_specs=[row(w) for w, _ in outs],
        out_shape=[jax.ShapeDtypeStruct((t, w), d) for w, d in outs],
        compiler_params=_params("parallel"),
        name="ab_in",
    )(x, ng, wm, wl, wf, up, bias)


def _gla_local(q_ref, k_ref, v_ref, g_ref, *, chunks, reverse):
    c = GLA_CHUNK
    n = chunks * c
    pair_w = 2 * GLA_DK
    ri = lax.broadcasted_iota(jnp.int32, (n, n), 0)
    ci = lax.broadcasted_iota(jnp.int32, (n, n), 1)
    causal = (ri <= ci) if reverse else (ri >= ci)
    tri = jnp.where(causal & ((ri // c) == (ci // c)), 1.0, 0.0).astype(BF16)
    g_hi, g_lo = _split_bf16(g_ref[0])
    bc = _dot(tri, g_hi) + _dot(tri, g_lo)
    q_in = (q_ref[0].astype(F32) * jnp.exp(bc)).astype(BF16)
    k_in = k_ref[0].astype(F32) * jnp.exp(-bc)
    v = v_ref[0]
    last = [j * c if reverse else (j + 1) * c - 1 for j in range(chunks)]
    tot = jnp.concatenate([bc[t:t + 1, :] for t in last] + [jnp.zeros((pair_w - chunks, GLA_QK), F32)],
                          axis=0)
    e_row = jnp.exp(tot)
    e_col = jnp.exp(tot.T)
    r2 = lax.broadcasted_iota(jnp.int32, (c, pair_w), 0)
    c2 = lax.broadcasted_iota(jnp.int32, (c, pair_w), 1) % c
    keep = (r2 <= c2) if reverse else (r2 >= c2)
    first = lax.broadcasted_iota(jnp.int32, (1, pair_w), 1) < GLA_DK
    rb = lax.broadcasted_iota(jnp.int32, (pair_w, 2 * GLA_DV), 0) < GLA_DK
    cb = lax.broadcasted_iota(jnp.int32, (pair_w, 2 * GLA_DV), 1) < GLA_DV
    diag = rb == cb
    local = {}
    for j in range(chunks):
        rows = slice(j * c, (j + 1) * c)
        k_c = k_in[rows]
        k_dec = (k_c * e_row[j:j + 1, :]).astype(BF16)
        for p in range(GLA_HEADS // 2):
            pair = slice(pair_w * p, pair_w * (p + 1))
            kp = k_c[:, pair]
            k_bd = jnp.concatenate([jnp.where(first, kp, 0.0), jnp.where(first, 0.0, kp)], axis=0).astype(BF16)
            qp = q_in[rows, pair]
            att = lax.dot_general(qp, k_bd, _NT, preferred_element_type=F32)
            att = jnp.where(keep, att, 0.0).astype(BF16)
            vp = v[rows, 2 * GLA_DV * p:2 * GLA_DV * (p + 1)]
            v_bd = jnp.where(diag, jnp.concatenate([vp, vp], axis=0), jnp.zeros((), BF16))
            upd = lax.dot_general(k_dec[:, pair], vp, _TN, preferred_element_type=F32)
            local[j, p] = (jnp.concatenate([att, qp], axis=1), v_bd, jnp.where(diag, upd, 0.0),
                           e_col[pair, j:j + 1])
    return local


def _gla_serial(local, o_ref, state_ref, *, chunks, reverse):
    c = GLA_CHUNK
    states = [state_ref[p] for p in range(GLA_HEADS // 2)]
    for j in (reversed(range(chunks)) if reverse else range(chunks)):
        outs = []
        for p in range(GLA_HEADS // 2):
            lhs, v_bd, upd, decay = local[j, p]
            outs.append(_dot(lhs, jnp.concatenate([v_bd, states[p].astype(BF16)], axis=0)))
            states[p] = states[p] * decay + upd
        o_ref[0, j * c:(j + 1) * c, :] = jnp.concatenate(outs, axis=-1)
    for p in range(GLA_HEADS // 2):
        state_ref[p] = states[p]


def _gla_kernel(qf_ref, kf_ref, vf_ref, gf_ref, qb_ref, kb_ref, vb_ref, gb_ref,
                of_ref, ob_ref, sf_ref, sb_ref, *, chunks):
    @pl.when(pl.program_id(1) == 0)
    def _():
        sf_ref[...] = jnp.zeros_like(sf_ref)
        sb_ref[...] = jnp.zeros_like(sb_ref)

    fwd = _gla_local(qf_ref, kf_ref, vf_ref, gf_ref, chunks=chunks, reverse=False)
    bwd = _gla_local(qb_ref, kb_ref, vb_ref, gb_ref, chunks=chunks, reverse=True)
    _gla_serial(fwd, of_ref, sf_ref, chunks=chunks, reverse=False)
    _gla_serial(bwd, ob_ref, sb_ref, chunks=chunks, reverse=True)


def _gla(q, k, v, gf, gb, blk):
    b, s, _ = q.shape
    nb = s // blk
    fwd = lambda w: pl.BlockSpec((1, blk, w), lambda i, n: (i, n, 0))
    bwd = lambda w: pl.BlockSpec((1, blk, w), lambda i, n: (i, nb - 1 - n, 0))
    state = pltpu.VMEM((GLA_HEADS // 2, 2 * GLA_DK, 2 * GLA_DV), F32)
    return pl.pallas_call(
        functools.partial(_gla_kernel, chunks=blk // GLA_CHUNK),
        grid=(b, nb),
        in_specs=[fwd(GLA_QK), fwd(GLA_QK), fwd(GLA_V), fwd(GLA_QK),
                  bwd(GLA_QK), bwd(GLA_QK), bwd(GLA_V), bwd(GLA_QK)],
        out_specs=[fwd(GLA_V), bwd(GLA_V)],
        out_shape=[jax.ShapeDtypeStruct((b, s, GLA_V), F32)] * 2,
        scratch_shapes=[state, state],
        compiler_params=_params("parallel", "arbitrary"),
        name="gla",
    )(q, k, v, gf, q, k, v, gb)


def _fnet_kernel(x_ref, ww_ref, k1c_ref, k1s_ref, k2c_ref, k2s_ref, twc_ref, tws_ref,
                 o_ref, z_ref, a_ref, *, nl, nm, groups):
    gw = FNET_GW
    cw = groups * gw
    rows_per = SUBLANES * nm
    for i in range(nl // SUBLANES):
        xs = x_ref[0, pl.ds(i * rows_per, rows_per), :]
        zr, zi = [], []
        for g in range(groups):
            zz = _dot(xs[:, g * gw:(g + 1) * gw], ww_ref[...])
            zr.append(zz[:, :gw])
            zi.append(zz[:, gw:])
        zcat = jnp.concatenate(zr + zi, axis=-1)
        z_ref[pl.ds(i * SUBLANES, SUBLANES)] = zcat.reshape(SUBLANES, nm // SUBLANES, SUBLANES, 2 * cw)
    for mb in range(nm // SUBLANES):
        z = z_ref[:, mb].reshape(nl * SUBLANES, 2 * cw)
        zr, zi = z[:, :cw], z[:, cw:]
        za = z.astype(BF16)
        zb = jnp.concatenate([zi, -zr], axis=-1).astype(BF16)
        a = _dot(k1c_ref[...], za) + _dot(k1s_ref[...], zb)
        ar, ai = a[:, :cw], a[:, cw:]
        tc = twc_ref[:, mb].reshape(nl * SUBLANES, gw)
        ts = tws_ref[:, mb].reshape(nl * SUBLANES, gw)
        tc = jnp.concatenate([tc] * groups, axis=-1)
        ts = jnp.concatenate([ts] * groups, axis=-1)
        pr = ar * tc + ai * ts
        pi = ai * tc - ar * ts
        a_ref[:, mb] = jnp.concatenate([pr, pi], axis=-1).reshape(nl, SUBLANES, 2 * cw)
    for ab in range(nl // SUBLANES):
        bb = a_ref[pl.ds(ab * SUBLANES, SUBLANES)].reshape(SUBLANES * nm, 2 * cw).astype(BF16)
        res = _dot(k2c_ref[...], bb[:, :cw]) + _dot(k2s_ref[...], bb[:, cw:])
        o_ref[0, :, ab] = res.reshape(nm, SUBLANES, cw)


def _dft_tables(s):
    nm = 64
    nl = s // nm
    assert nl * nm == s and nl % SUBLANES == 0
    eye = np.eye(SUBLANES)

    def dft(n):
        idx = np.arange(n)
        ang = 2.0 * np.pi * ((idx[:, None] * idx[None, :]) % n) / n
        return np.cos(ang), np.sin(ang)

    k1c, k1s = (np.kron(f, eye) for f in dft(nl))
    k2c, k2s = (np.einsum("nm,ij->nijm", f, eye).reshape(nm * SUBLANES, SUBLANES * nm)
                for f in dft(nm))
    a = np.arange(nl)[:, None]
    m = np.arange(nm)[None, :]
    ang = 2.0 * np.pi * ((a * m) % s) / s
    norm = 1.0 / np.sqrt(float(s) * FNET_GW)
    shape = (nl, nm // SUBLANES, SUBLANES, FNET_GW)
    twc = np.broadcast_to((np.cos(ang) * norm).reshape(nl, nm // SUBLANES, SUBLANES, 1), shape)
    tws = np.broadcast_to((np.sin(ang) * norm).reshape(nl, nm // SUBLANES, SUBLANES, 1), shape)
    w = np.arange(FNET_GW)
    angw = 2.0 * np.pi * ((w[:, None] * w[None, :]) % FNET_GW) / FNET_GW
    ww = np.concatenate([np.cos(angw), -np.sin(angw)], axis=1)
    bf = lambda v: jnp.asarray(v, dtype=F32).astype(BF16)
    return (nl, nm, bf(ww), bf(k1c), bf(k1s), bf(k2c), bf(k2s),
            jnp.asarray(twc, dtype=F32), jnp.asarray(tws, dtype=F32))


def _fnet(f, groups):
    b, s, _ = f.shape
    nl, nm, ww, k1c, k1s, k2c, k2s, twc, tws = _dft_tables(s)
    cw = groups * FNET_GW
    consts = (ww, k1c, k1s, k2c, k2s, twc, tws)
    out = pl.pallas_call(
        functools.partial(_fnet_kernel, nl=nl, nm=nm, groups=groups),
        grid=(b, FNET_W // cw),
        in_specs=[pl.BlockSpec((1, s, cw), lambda i, j: (i, 0, j))]
                 + [_const_spec(a.shape) for a in consts],
        out_specs=pl.BlockSpec((1, nm, nl // SUBLANES, SUBLANES, cw), lambda i, j: (i, 0, 0, 0, j)),
        out_shape=jax.ShapeDtypeStruct((b, nm, nl // SUBLANES, SUBLANES, FNET_W), F32),
        scratch_shapes=[pltpu.VMEM((nl, nm // SUBLANES, SUBLANES, 2 * cw), F32),
                        pltpu.VMEM((nl, nm // SUBLANES, SUBLANES, 2 * cw), F32)],
        compiler_params=_params("parallel", "parallel"),
        name="fnet",
    )(f, *consts)
    return out.reshape(b, s, FNET_W)


FFN_SPLIT = 6 * 256


def _swiglu(x, ng_ref, wg_ref, wu_ref, wd_ref):
    h = _rms(x, ng_ref[...]).astype(BF16)
    y = x
    for cols in (slice(0, FFN_SPLIT), slice(FFN_SPLIT, FFN_HIDDEN)):
        g = _dot(h, wg_ref[:, cols])
        u = _dot(h, wu_ref[:, cols])
        a = (g * jax.nn.sigmoid(g) * u).astype(BF16)
        y = y + _dot(a, wd_ref[cols, :])
    return y


def _ab_out_ffn_kernel(of_ref, ob_ref, r_ref, fo_ref, x_ref, og_ref, wo_ref, wf_ref,
                       ng_ref, wg_ref, wu_ref, wd_ref, y_ref):
    o = of_ref[...] + ob_ref[...]
    heads = [_rms(o[:, GLA_DV * h:GLA_DV * (h + 1)], og_ref[...]) for h in range(GLA_HEADS)]
    r = r_ref[...].astype(F32)
    o = jnp.concatenate(heads, axis=-1) * (r * jax.nn.sigmoid(r))
    mixed = _dot(o.astype(BF16), wo_ref[...]) + _dot(fo_ref[...].astype(BF16), wf_ref[...])
    y_ref[...] = _swiglu(x_ref[...] + mixed, ng_ref, wg_ref, wu_ref, wd_ref)


def _c_out_ffn_kernel(o_ref, x_ref, w_ref, ng_ref, wg_ref, wu_ref, wd_ref, y_ref):
    x = x_ref[...] + _dot(o_ref[...], w_ref[...])
    y_ref[...] = _swiglu(x, ng_ref, wg_ref, wu_ref, wd_ref)


def _rows_call(body, name, row_args, const_args, tm):
    t = row_args[0].shape[0]
    row = lambda a: pl.BlockSpec((tm, a.shape[1]), lambda i: (i, 0))
    return pl.pallas_call(
        body,
        grid=(t // tm,),
        in_specs=[row(a) for a in row_args] + [_const_spec(a.shape) for a in const_args],
        out_specs=pl.BlockSpec((tm, D_MODEL), lambda i: (i, 0)),
        out_shape=jax.ShapeDtypeStruct((t, D_MODEL), F32),
        compiler_params=_params("parallel"),
        name=name,
    )(*row_args, *const_args)


_QUARTER = HEAD_DIM // 4
_HEAD_PERM = np.concatenate([np.arange(0, _QUARTER), np.arange(2 * _QUARTER, 3 * _QUARTER),
                             np.arange(_QUARTER, 2 * _QUARTER), np.arange(3 * _QUARTER, 4 * _QUARTER)])


def _qk_norm_rope(y2, ones_bd, cos, sin):
    sq = (y2 * y2).astype(BF16)
    ms = _dot(sq, ones_bd) * (1.0 / HEAD_DIM)
    inv = lax.rsqrt(ms + NORM_EPS)
    halves = []
    for j in range(2):
        yh = y2[:, HEAD_DIM * j:HEAD_DIM * (j + 1)]
        halves.append(yh * cos + pltpu.roll(yh, HEAD_DIM // 2, 1) * sin)
    return jnp.concatenate(halves, axis=-1) * inv


def _c_in_kernel(x_ref, ng_ref, w_ref, bd_ref, cq_ref, sq_ref, ck_ref, sk_ref, q_ref, k_ref, v_ref):
    h = _rms(x_ref[...], ng_ref[...]).astype(BF16)
    y = _dot(h, w_ref[...])
    ones_bd = bd_ref[...]
    cq, sq, ck, sk = cq_ref[...], sq_ref[...], ck_ref[...], sk_ref[...]
    for i in range(0, ATTN_HEADS, 2):
        cols = slice(HEAD_DIM * i, HEAD_DIM * (i + 2))
        q_ref[:, cols] = _qk_norm_rope(y[:, cols], ones_bd, cq, sq).astype(BF16)
    k_ref[...] = _qk_norm_rope(y[:, ATTN_Q:ATTN_Q + ATTN_KV], ones_bd, ck, sk).astype(BF16)
    ones = jnp.ones((y.shape[0], HEAD_DIM), BF16)
    for i in range(KV_HEADS):
        vh = y[:, ATTN_Q + ATTN_KV + HEAD_DIM * i:ATTN_Q + ATTN_KV + HEAD_DIM * (i + 1)]
        v_ref[:, 2 * HEAD_DIM * i:2 * HEAD_DIM * (i + 1)] = jnp.concatenate([vh.astype(BF16), ones], axis=-1)


def _rope_tables(s, qg, kg):
    pos = jnp.arange(s)
    row = (pos // GRID_W).astype(F32)[:, None]
    col = (pos % GRID_W).astype(F32)[:, None]
    half = HEAD_DIM // 2
    inv = ROPE_THETA ** (-jnp.arange(0, half, 2, dtype=F32) / half)
    ar, ac = row * inv[None, :], col * inv[None, :]
    ang = jnp.concatenate([ar, ac, ar, ac], axis=-1)
    cos = jnp.cos(ang)
    sin = jnp.sin(ang) * jnp.where(jnp.arange(HEAD_DIM) < half, -1.0, 1.0)[None, :]
    q_scale = (HEAD_DIM ** -0.5) * float(np.log2(np.e))
    out = []
    for gain, scale in ((qg, q_scale), (kg, 1.0)):
        gp = gain[_HEAD_PERM] * scale
        out += [cos * gp[None, :], sin * jnp.roll(gp, half)[None, :]]
    return out


def _c_in(x, ng, w, tables, s, tm):
    t = x.shape[0]
    per_seq = s // tm
    row = lambda wd: pl.BlockSpec((tm, wd), lambda i: (i, 0))
    tab = pl.BlockSpec((tm, HEAD_DIM), lambda i: (i % per_seq, 0))
    outs = [(ATTN_Q, BF16), (ATTN_KV, BF16), (2 * ATTN_KV, BF16)]
    ones_bd = jnp.asarray(np.kron(np.eye(2), np.ones((HEAD_DIM, HEAD_DIM))), dtype=BF16)
    return pl.pallas_call(
        _c_in_kernel,
        grid=(t // tm,),
        in_specs=[row(D_MODEL)] + [_const_spec(a.shape) for a in (ng, w, ones_bd)] + [tab] * 4,
        out_specs=[row(wd) for wd, _ in outs],
        out_shape=[jax.ShapeDtypeStruct((t, wd), d) for wd, d in outs],
        compiler_params=_params("parallel"),
        name="c_in",
    )(x, ng, w, ones_bd, *tables)


def _attn_kernel(q_ref, k_ref, v_ref, o_ref, s_ref, *, tq):
    group = ATTN_HEADS // KV_HEADS
    nblk = q_ref.shape[1] // tq

    def rows_of(r):
        return pl.ds(pl.multiple_of(r * tq, tq), tq)

    def scores(r, g, buf):
        q = q_ref[0, rows_of(r), HEAD_DIM * g:HEAD_DIM * (g + 1)]
        s_ref[buf] = lax.dot_general(q, k_ref[0], _NT, preferred_element_type=F32)

    def finish(r, g, buf):
        sc = s_ref[buf]
        m = jnp.max(sc, axis=-1, keepdims=True)
        ov = _dot(jnp.exp2(sc - m).astype(BF16), v_ref[0])
        out = ov[:, :HEAD_DIM] / ov[:, HEAD_DIM:]
        o_ref[0, rows_of(r), HEAD_DIM * g:HEAD_DIM * (g + 1)] = out.astype(o_ref.dtype)

    scores(0, 0, 0)

    def body(r, carry):
        for g in range(group):
            if g + 1 < group:
                scores(r, g + 1, (g + 1) % 2)
            else:
                scores(jnp.minimum(r + 1, nblk - 1), 0, 0)
            finish(r, g, g % 2)
        return carry

    lax.fori_loop(0, nblk, body, 0)


def _attn(q, k, v, tq):
    b, s, _ = q.shape
    gw = ATTN_Q // KV_HEADS
    return pl.pallas_call(
        functools.partial(_attn_kernel, tq=tq),
        grid=(b, KV_HEADS),
        in_specs=[pl.BlockSpec((1, s, gw), lambda i, j: (i, 0, j)),
                  pl.BlockSpec((1, s, HEAD_DIM), lambda i, j: (i, 0, j)),
                  pl.BlockSpec((1, s, 2 * HEAD_DIM), lambda i, j: (i, 0, j))],
        out_specs=pl.BlockSpec((1, s, gw), lambda i, j: (i, 0, j)),
        out_shape=jax.ShapeDtypeStruct((b, s, ATTN_Q), BF16),
        scratch_shapes=[pltpu.VMEM((2, tq, s), F32)],
        compiler_params=_params("parallel", "parallel"),
        name="attn",
    )(q, k, v)


def _prep_weights(ab_norm, ab_w_in, gla_up_f, gla_bias_f, gla_up_b, gla_bias_b, gla_out_norm,
                  ab_w_out, c_norm, c_w_in, c_q_norm, c_k_norm, c_w_out, ffn_norm,
                  ffn_w_gate, ffn_w_up, ffn_w_down):
    n_main = 2 * GLA_QK + 2 * GLA_V
    w_in = ab_w_in[0]
    zeros = jnp.zeros((GLA_RANK, GLA_QK), F32)
    up = jnp.concatenate([jnp.concatenate([gla_up_f[0], zeros], axis=1),
                          jnp.concatenate([zeros, gla_up_b[0]], axis=1)], axis=0)
    qk_heads = ATTN_HEADS + KV_HEADS
    cols = np.concatenate([(HEAD_DIM * np.arange(qk_heads)[:, None] + _HEAD_PERM[None, :]).reshape(-1),
                           np.arange(HEAD_DIM * qk_heads, HEAD_DIM * (qk_heads + KV_HEADS))])
    return dict(
        ab_ng=ab_norm[0][None, :],
        ab_wm=w_in[:, :n_main].astype(BF16),
        ab_wl=w_in[:, n_main:n_main + 2 * GLA_RANK].astype(BF16),
        ab_wf=w_in[:, n_main + 2 * GLA_RANK:].astype(BF16),
        ab_up=up,
        ab_bias=jnp.concatenate([gla_bias_f[0], gla_bias_b[0]])[None, :],
        ab_og=gla_out_norm[0][None, :],
        ab_wo=ab_w_out[0][:GLA_V].astype(BF16),
        ab_wfo=ab_w_out[0][GLA_V:].astype(BF16),
        c_ng=c_norm[0][None, :],
        c_w=c_w_in[0][:, cols].astype(BF16),
        c_qg=c_q_norm[0],
        c_kg=c_k_norm[0],
        c_wo=c_w_out[0].astype(BF16),
        ffn_ng=[ffn_norm[i][None, :] for i in range(2)],
        ffn_wg=[ffn_w_gate[i].astype(BF16) for i in range(2)],
        ffn_wu=[ffn_w_up[i].astype(BF16) for i in range(2)],
        ffn_wd=[ffn_w_down[i].astype(BF16) for i in range(2)],
    )


def _trunk(x, w, tm=512, tm_ffn=512, gla_blk=256, tq=256, fnet_groups=2):
    b, s, d = x.shape
    t = b * s
    xf = x.reshape(t, d)
    q, k, v, r, gf, gb, f = _ab_in(xf, w["ab_ng"], w["ab_wm"], w["ab_wl"], w["ab_wf"],
                                   w["ab_up"], w["ab_bias"], tm)
    seq = lambda a: a.reshape(b, s, a.shape[-1])
    o_f, o_b = _gla(seq(q), seq(k), seq(v), seq(gf), seq(gb), gla_blk)
    fo = _fnet(seq(f), fnet_groups)
    ffn = lambda i: (w["ffn_ng"][i], w["ffn_wg"][i], w["ffn_wu"][i], w["ffn_wd"][i])
    xf = _rows_call(_ab_out_ffn_kernel, "ab_out_ffn",
                    (o_f.reshape(t, GLA_V), o_b.reshape(t, GLA_V), r, fo.reshape(t, FNET_W), xf),
                    (w["ab_og"], w["ab_wo"], w["ab_wfo"]) + ffn(0), tm_ffn)
    q, k, v = _c_in(xf, w["c_ng"], w["c_w"], _rope_tables(s, w["c_qg"], w["c_kg"]), s, tm)
    o = _attn(seq(q), seq(k), seq(v), tq)
    xf = _rows_call(_c_out_ffn_kernel, "c_out_ffn", (o.reshape(t, ATTN_Q), xf), (w["c_wo"],) + ffn(1), tm_ffn)
    return xf.reshape(b, s, d)


def kernel(x_prompt, x_sample, ab_norm, ab_w_in, gla_up_f, gla_bias_f, gla_up_b, gla_bias_b,
           gla_out_norm, ab_w_out, c_norm, c_w_in, c_q_norm, c_k_norm, c_w_out, ffn_norm,
           ffn_w_gate, ffn_w_up, ffn_w_down):
    w = _prep_weights(ab_norm, ab_w_in, gla_up_f, gla_bias_f, gla_up_b, gla_bias_b, gla_out_norm,
                      ab_w_out, c_norm, c_w_in, c_q_norm, c_k_norm, c_w_out, ffn_norm,
                      ffn_w_gate, ffn_w_up, ffn_w_down)
    return _trunk(x_prompt, w), _trunk(x_sample, w)
```

```python
import functools

import numpy as np
import jax
import jax.numpy as jnp
from jax import lax
from jax.experimental import pallas as pl
from jax.experimental.pallas import tpu as pltpu

D_MODEL = 1024
GRID_W = 64
NORM_EPS = 1e-6
GLA_HEADS = 4
GLA_DK = 64
GLA_DV = 128
GLA_RANK = 16
GLA_GATE_NORM = 16.0
GLA_CHUNK = 64
FNET_GROUPS = 4
FNET_GW = 128
ATTN_HEADS = 8
KV_HEADS = 2
HEAD_DIM = 128
ROPE_THETA = 10000.0
FFN_HIDDEN = -(-8 * D_MODEL // (3 * 256)) * 256

GLA_QK = GLA_HEADS * GLA_DK
GLA_V = GLA_HEADS * GLA_DV
FNET_W = FNET_GROUPS * FNET_GW
ATTN_Q = ATTN_HEADS * HEAD_DIM
ATTN_KV = KV_HEADS * HEAD_DIM

SUBLANES = 8
VMEM_LIMIT = 56 * 1024 * 1024

BF16 = jnp.bfloat16
F32 = jnp.float32

_NT = (((1,), (1,)), ((), ()))
_TN = (((0,), (0,)), ((), ()))


def _params(*sem):
    return pltpu.CompilerParams(dimension_semantics=sem, vmem_limit_bytes=VMEM_LIMIT)


def _const_spec(shape):
    nd = len(shape)
    return pl.BlockSpec(shape, lambda *_: (0,) * nd, pipeline_mode=pl.Buffered(1))


def _dot(a, b):
    return jnp.dot(a, b, preferred_element_type=F32)


def _split_bf16(a):
    hi = a.astype(BF16)
    lo = (a - hi.astype(F32)).astype(BF16)
    return hi, lo


def _row_groups(ref, groups=2):
    n = ref.shape[0] // groups
    return [pl.ds(i * n, n) for i in range(groups)]


def _rms(x, gain):
    return x * lax.rsqrt(jnp.mean(x * x, axis=-1, keepdims=True) + NORM_EPS) * gain


def _ab_in_kernel(x_ref, ng_ref, wm_ref, wl_ref, wf_ref, up_ref, bias_ref,
                  q_ref, k_ref, v_ref, r_ref, gf_ref, gb_ref, f_ref):
    h = _rms(x_ref[...], ng_ref[...]).astype(BF16)
    low = _dot(h, wl_ref[...])
    y = _dot(h, wm_ref[...])
    q_ref[...] = (y[:, :GLA_QK] * (GLA_DK ** -0.5)).astype(BF16)
    k_ref[...] = y[:, GLA_QK:2 * GLA_QK].astype(BF16)
    v_ref[...] = y[:, 2 * GLA_QK:2 * GLA_QK + GLA_V].astype(BF16)
    r_ref[...] = y[:, 2 * GLA_QK + GLA_V:].astype(BF16)
    low_hi, low_lo = _split_bf16(low)
    up_hi, up_lo = _split_bf16(up_ref[...])
    z = _dot(low_hi, up_hi) + _dot(low_hi, up_lo) + _dot(low_lo, up_hi) + bias_ref[...]
    g = (jnp.minimum(z, 0.0) - jnp.log(1.0 + jnp.exp(-jnp.abs(z)))) * (1.0 / GLA_GATE_NORM)
    gf_ref[...] = g[:, :GLA_QK]
    gb_ref[...] = g[:, GLA_QK:]
    f_ref[...] = _dot(h, wf_ref[...]).astype(BF16)


def _ab_in(x, ng, wm, wl, wf, up, bias, tm):
    t = x.shape[0]
    row = lambda w: pl.BlockSpec((tm, w), lambda i: (i, 0))
    outs = [(GLA_QK, BF16), (GLA_QK, BF16), (GLA_V, BF16), (GLA_V, BF16),
            (GLA_QK, F32), (GLA_QK, F32), (FNET_W, BF16)]
    return pl.pallas_call(
        _ab_in_kernel,
        grid=(t // tm,),
        in_specs=[row(D_MODEL)] + [_const_spec(a.shape) for a in (ng, wm, wl, wf, up, bias)],
        out_specs=[row(w) for w, _ in outs],
        out_shape=[jax.ShapeDtypeStruct((t, w), d) for w, d in outs],
        compiler_params=_params("parallel"),
        name="ab_in",
    )(x, ng, wm, wl, wf, up, bias)


def _gla_local(q_ref, k_ref, v_ref, g_ref, *, chunks, reverse):
    c = GLA_CHUNK
    n = chunks * c
    pair_w = 2 * GLA_DK
    ri = lax.broadcasted_iota(jnp.int32, (n, n), 0)
    ci = lax.broadcasted_iota(jnp.int32, (n, n), 1)
    causal = (ri <= ci) if reverse else (ri >= ci)
    tri = jnp.where(causal & ((ri // c) == (ci // c)), 1.0, 0.0).astype(BF16)
    g_hi, g_lo = _split_bf16(g_ref[0])
    bc = _dot(tri, g_hi) + _dot(tri, g_lo)
    q_in = (q_ref[0].astype(F32) * jnp.exp(bc)).astype(BF16)
    k_in = k_ref[0].astype(F32) * jnp.exp(-bc)
    v = v_ref[0]
    last = [j * c if reverse else (j + 1) * c - 1 for j in range(chunks)]
    tot = jnp.concatenate([bc[t:t + 1, :] for t in last] + [jnp.zeros((pair_w - chunks, GLA_QK), F32)],
                          axis=0)
    e_row = jnp.exp(tot)
    e_col = jnp.exp(tot.T)
    r2 = lax.broadcasted_iota(jnp.int32, (c, pair_w), 0)
    c2 = lax.broadcasted_iota(jnp.int32, (c, pair_w), 1) % c
    keep = (r2 <= c2) if reverse else (r2 >= c2)
    first = lax.broadcasted_iota(jnp.int32, (1, pair_w), 1) < GLA_DK
    rb = lax.broadcasted_iota(jnp.int32, (pair_w, 2 * GLA_DV), 0) < GLA_DK
    cb = lax.broadcasted_iota(jnp.int32, (pair_w, 2 * GLA_DV), 1) < GLA_DV
    diag = rb == cb
    local = {}
    for j in range(chunks):
        rows = slice(j * c, (j + 1) * c)
        k_c = k_in[rows]
        k_dec = (k_c * e_row[j:j + 1, :]).astype(BF16)
        for p in range(GLA_HEADS // 2):
            pair = slice(pair_w * p, pair_w * (p + 1))
            kp = k_c[:, pair]
            k_bd = jnp.concatenate([jnp.where(first, kp, 0.0), jnp.where(first, 0.0, kp)], axis=0).astype(BF16)
            qp = q_in[rows, pair]
            att = lax.dot_general(qp, k_bd, _NT, preferred_element_type=F32)
            att = jnp.where(keep, att, 0.0).astype(BF16)
            vp = v[rows, 2 * GLA_DV * p:2 * GLA_DV * (p + 1)]
            v_bd = jnp.where(diag, jnp.concatenate([vp, vp], axis=0), jnp.zeros((), BF16))
            upd = lax.dot_general(k_dec[:, pair], vp, _TN, preferred_element_type=F32)
            local[j, p] = (jnp.concatenate([att, qp], axis=1), v_bd, jnp.where(diag, upd, 0.0),
                           e_col[pair, j:j + 1])
    return local


def _gla_serial(local, o_ref, state_ref, *, chunks, reverse):
    c = GLA_CHUNK
    states = [state_ref[p] for p in range(GLA_HEADS // 2)]
    for j in (reversed(range(chunks)) if reverse else range(chunks)):
        outs = []
        for p in range(GLA_HEADS // 2):
            lhs, v_bd, upd, decay = local[j, p]
            outs.append(_dot(lhs, jnp.concatenate([v_bd, states[p].astype(BF16)], axis=0)))
            states[p] = states[p] * decay + upd
        o_ref[0, j * c:(j + 1) * c, :] = jnp.concatenate(outs, axis=-1)
    for p in range(GLA_HEADS // 2):
        state_ref[p] = states[p]


def _gla_kernel(qf_ref, kf_ref, vf_ref, gf_ref, qb_ref, kb_ref, vb_ref, gb_ref,
                of_ref, ob_ref, sf_ref, sb_ref, *, chunks):
    @pl.when(pl.program_id(1) == 0)
    def _():
        sf_ref[...] = jnp.zeros_like(sf_ref)
        sb_ref[...] = jnp.zeros_like(sb_ref)

    fwd = _gla_local(qf_ref, kf_ref, vf_ref, gf_ref, chunks=chunks, reverse=False)
    bwd = _gla_local(qb_ref, kb_ref, vb_ref, gb_ref, chunks=chunks, reverse=True)
    _gla_serial(fwd, of_ref, sf_ref, chunks=chunks, reverse=False)
    _gla_serial(bwd, ob_ref, sb_ref, chunks=chunks, reverse=True)


def _gla(q, k, v, gf, gb, blk):
    b, s, _ = q.shape
    nb = s // blk
    fwd = lambda w: pl.BlockSpec((1, blk, w), lambda i, n: (i, n, 0))
    bwd = lambda w: pl.BlockSpec((1, blk, w), lambda i, n: (i, nb - 1 - n, 0))
    state = pltpu.VMEM((GLA_HEADS // 2, 2 * GLA_DK, 2 * GLA_DV), F32)
    return pl.pallas_call(
        functools.partial(_gla_kernel, chunks=blk // GLA_CHUNK),
        grid=(b, nb),
        in_specs=[fwd(GLA_QK), fwd(GLA_QK), fwd(GLA_V), fwd(GLA_QK),
                  bwd(GLA_QK), bwd(GLA_QK), bwd(GLA_V), bwd(GLA_QK)],
        out_specs=[fwd(GLA_V), bwd(GLA_V)],
        out_shape=[jax.ShapeDtypeStruct((b, s, GLA_V), F32)] * 2,
        scratch_shapes=[state, state],
        compiler_params=_params("parallel", "arbitrary"),
        name="gla",
    )(q, k, v, gf, q, k, v, gb)


def _fnet_kernel(x_ref, ww_ref, k1c_ref, k1s_ref, k2c_ref, k2s_ref, twc_ref, tws_ref,
                 o_ref, z_ref, a_ref, *, nl, nm, groups):
    gw = FNET_GW
    cw = groups * gw
    rows_per = SUBLANES * nm
    for i in range(nl // SUBLANES):
        xs = x_ref[0, pl.ds(i * rows_per, rows_per), :]
        zr, zi = [], []
        for g in range(groups):
            zz = _dot(xs[:, g * gw:(g + 1) * gw], ww_ref[...])
            zr.append(zz[:, :gw])
            zi.append(zz[:, gw:])
        zcat = jnp.concatenate(zr + zi, axis=-1)
        z_ref[pl.ds(i * SUBLANES, SUBLANES)] = zcat.reshape(SUBLANES, nm // SUBLANES, SUBLANES, 2 * cw)
    for mb in range(nm // SUBLANES):
        z = z_ref[:, mb].reshape(nl * SUBLANES, 2 * cw)
        zr, zi = z[:, :cw], z[:, cw:]
        za = z.astype(BF16)
        zb = jnp.concatenate([zi, -zr], axis=-1).astype(BF16)
        a = _dot(k1c_ref[...], za) + _dot(k1s_ref[...], zb)
        ar, ai = a[:, :cw], a[:, cw:]
        tc = twc_ref[:, mb].reshape(nl * SUBLANES, gw)
        ts = tws_ref[:, mb].reshape(nl * SUBLANES, gw)
        tc = jnp.concatenate([tc] * groups, axis=-1)
        ts = jnp.concatenate([ts] * groups, axis=-1)
        pr = ar * tc + ai * ts
        pi = ai * tc - ar * ts
        a_ref[:, mb] = jnp.concatenate([pr, pi], axis=-1).reshape(nl, SUBLANES, 2 * cw)
    for ab in range(nl // SUBLANES):
        bb = a_ref[pl.ds(ab * SUBLANES, SUBLANES)].reshape(SUBLANES * nm, 2 * cw).astype(BF16)
        res = _dot(k2c_ref[...], bb[:, :cw]) + _dot(k2s_ref[...], bb[:, cw:])
        o_ref[0, :, ab] = res.reshape(nm, SUBLANES, cw)


def _dft_tables(s):
    nm = 64
    nl = s // nm
    assert nl * nm == s and nl % SUBLANES == 0
    eye = np.eye(SUBLANES)

    def dft(n):
        idx = np.arange(n)
        ang = 2.0 * np.pi * ((idx[:, None] * idx[None, :]) % n) / n
        return np.cos(ang), np.sin(ang)

    k1c, k1s = (np.kron(f, eye) for f in dft(nl))
    k2c, k2s = (np.einsum("nm,ij->nijm", f, eye).reshape(nm * SUBLANES, SUBLANES * nm)
                for f in dft(nm))
    a = np.arange(nl)[:, None]
    m = np.arange(nm)[None, :]
    ang = 2.0 * np.pi * ((a * m) % s) / s
    norm = 1.0 / np.sqrt(float(s) * FNET_GW)
    shape = (nl, nm // SUBLANES, SUBLANES, FNET_GW)
    twc = np.broadcast_to((np.cos(ang) * norm).reshape(nl, nm // SUBLANES, SUBLANES, 1), shape)
    tws = np.broadcast_to((np.sin(ang) * norm).reshape(nl, nm // SUBLANES, SUBLANES, 1), shape)
    w = np.arange(FNET_GW)
    angw = 2.0 * np.pi * ((w[:, None] * w[None, :]) % FNET_GW) / FNET_GW
    ww = np.concatenate([np.cos(angw), -np.sin(angw)], axis=1)
    bf = lambda v: jnp.asarray(v, dtype=F32).astype(BF16)
    return (nl, nm, bf(ww), bf(k1c), bf(k1s), bf(k2c), bf(k2s),
            jnp.asarray(twc, dtype=F32), jnp.asarray(tws, dtype=F32))


def _fnet(f, groups):
    b, s, _ = f.shape
    nl, nm, ww, k1c, k1s, k2c, k2s, twc, tws = _dft_tables(s)
    cw = groups * FNET_GW
    consts = (ww, k1c, k1s, k2c, k2s, twc, tws)
    out = pl.pallas_call(
        functools.partial(_fnet_kernel, nl=nl, nm=nm, groups=groups),
        grid=(b, FNET_W // cw),
        in_specs=[pl.BlockSpec((1, s, cw), lambda i, j: (i, 0, j))]
                 + [_const_spec(a.shape) for a in consts],
        out_specs=pl.BlockSpec((1, nm, nl // SUBLANES, SUBLANES, cw), lambda i, j: (i, 0, 0, 0, j)),
        out_shape=jax.ShapeDtypeStruct((b, nm, nl // SUBLANES, SUBLANES, FNET_W), F32),
        scratch_shapes=[pltpu.VMEM((nl, nm // SUBLANES, SUBLANES, 2 * cw), F32),
                        pltpu.VMEM((nl, nm // SUBLANES, SUBLANES, 2 * cw), F32)],
        compiler_params=_params("parallel", "parallel"),
        name="fnet",
    )(f, *consts)
    return out.reshape(b, s, FNET_W)


FFN_SPLIT = 6 * 256


def _swiglu(xs, ng_ref, wg_ref, wu_ref, wd_ref):
    hs = [_rms(x, ng_ref[...]).astype(BF16) for x in xs]
    ys = list(xs)
    for cols in (slice(0, FFN_SPLIT), slice(FFN_SPLIT, FFN_HIDDEN)):
        gu = [(_dot(h, wg_ref[:, cols]), _dot(h, wu_ref[:, cols])) for h in hs]
        acts = [(g * jax.nn.sigmoid(g) * u).astype(BF16) for g, u in gu]
        ys = [y + _dot(a, wd_ref[cols, :]) for y, a in zip(ys, acts)]
    return ys


def _ab_out_ffn_kernel(of_ref, ob_ref, r_ref, fo_ref, x_ref, og_ref, wo_ref, wf_ref,
                       ng_ref, wg_ref, wu_ref, wd_ref, y_ref):
    groups = _row_groups(y_ref)
    xs = []
    for rows in groups:
        o = of_ref[rows, :] + ob_ref[rows, :]
        heads = [_rms(o[:, GLA_DV * h:GLA_DV * (h + 1)], og_ref[...]) for h in range(GLA_HEADS)]
        r = r_ref[rows, :].astype(F32)
        o = jnp.concatenate(heads, axis=-1) * (r * jax.nn.sigmoid(r))
        mixed = _dot(o.astype(BF16), wo_ref[...]) + _dot(fo_ref[rows, :].astype(BF16), wf_ref[...])
        xs.append(x_ref[rows, :] + mixed)
    for rows, y in zip(groups, _swiglu(xs, ng_ref, wg_ref, wu_ref, wd_ref)):
        y_ref[rows, :] = y


def _c_out_ffn_kernel(o_ref, x_ref, w_ref, ng_ref, wg_ref, wu_ref, wd_ref, y_ref):
    groups = _row_groups(y_ref)
    xs = [x_ref[rows, :] + _dot(o_ref[rows, :], w_ref[...]) for rows in groups]
    for rows, y in zip(groups, _swiglu(xs, ng_ref, wg_ref, wu_ref, wd_ref)):
        y_ref[rows, :] = y


def _rows_call(body, name, row_args, const_args, tm):
    t = row_args[0].shape[0]
    row = lambda a: pl.BlockSpec((tm, a.shape[1]), lambda i: (i, 0))
    return pl.pallas_call(
        body,
        grid=(t // tm,),
        in_specs=[row(a) for a in row_args] + [_const_spec(a.shape) for a in const_args],
        out_specs=pl.BlockSpec((tm, D_MODEL), lambda i: (i, 0)),
        out_shape=jax.ShapeDtypeStruct((t, D_MODEL), F32),
        compiler_params=_params("parallel"),
        name=name,
    )(*row_args, *const_args)


_QUARTER = HEAD_DIM // 4
_HEAD_PERM = np.concatenate([np.arange(0, _QUARTER), np.arange(2 * _QUARTER, 3 * _QUARTER),
                             np.arange(_QUARTER, 2 * _QUARTER), np.arange(3 * _QUARTER, 4 * _QUARTER)])


def _qk_norm_rope(y2, ones_bd, cos, sin):
    sq = (y2 * y2).astype(BF16)
    ms = _dot(sq, ones_bd) * (1.0 / HEAD_DIM)
    inv = lax.rsqrt(ms + NORM_EPS)
    halves = []
    for j in range(2):
        yh = y2[:, HEAD_DIM * j:HEAD_DIM * (j + 1)]
        halves.append(yh * cos + pltpu.roll(yh, HEAD_DIM // 2, 1) * sin)
    return jnp.concatenate(halves, axis=-1) * inv


def _c_in_kernel(x_ref, ng_ref, w_ref, bd_ref, cq_ref, sq_ref, ck_ref, sk_ref, q_ref, k_ref, v_ref):
    h = _rms(x_ref[...], ng_ref[...]).astype(BF16)
    y = _dot(h, w_ref[...])
    ones_bd = bd_ref[...]
    cq, sq, ck, sk = cq_ref[...], sq_ref[...], ck_ref[...], sk_ref[...]
    for i in range(0, ATTN_HEADS, 2):
        cols = slice(HEAD_DIM * i, HEAD_DIM * (i + 2))
        q_ref[:, cols] = _qk_norm_rope(y[:, cols], ones_bd, cq, sq).astype(BF16)
    k_ref[...] = _qk_norm_rope(y[:, ATTN_Q:ATTN_Q + ATTN_KV], ones_bd, ck, sk).astype(BF16)
    ones = jnp.ones((y.shape[0], HEAD_DIM), BF16)
    for i in range(KV_HEADS):
        vh = y[:, ATTN_Q + ATTN_KV + HEAD_DIM * i:ATTN_Q + ATTN_KV + HEAD_DIM * (i + 1)]
        v_ref[:, 2 * HEAD_DIM * i:2 * HEAD_DIM * (i + 1)] = jnp.concatenate([vh.astype(BF16), ones], axis=-1)


def _rope_tables(s, qg, kg):
    pos = jnp.arange(s)
    row = (pos // GRID_W).astype(F32)[:, None]
    col = (pos % GRID_W).astype(F32)[:, None]
    half = HEAD_DIM // 2
    inv = ROPE_THETA ** (-jnp.arange(0, half, 2, dtype=F32) / half)
    ar, ac = row * inv[None, :], col * inv[None, :]
    ang = jnp.concatenate([ar, ac, ar, ac], axis=-1)
    cos = jnp.cos(ang)
    sin = jnp.sin(ang) * jnp.where(jnp.arange(HEAD_DIM) < half, -1.0, 1.0)[None, :]
    q_scale = (HEAD_DIM ** -0.5) * float(np.log2(np.e))
    out = []
    for gain, scale in ((qg, q_scale), (kg, 1.0)):
        gp = gain[_HEAD_PERM] * scale
        out += [cos * gp[None, :], sin * jnp.roll(gp, half)[None, :]]
    return out


def _c_in(x, ng, w, tables, s, tm):
    t = x.shape[0]
    per_seq = s // tm
    row = lambda wd: pl.BlockSpec((tm, wd), lambda i: (i, 0))
    tab = pl.BlockSpec((tm, HEAD_DIM), lambda i: (i % per_seq, 0))
    outs = [(ATTN_Q, BF16), (ATTN_KV, BF16), (2 * ATTN_KV, BF16)]
    ones_bd = jnp.asarray(np.kron(np.eye(2), np.ones((HEAD_DIM, HEAD_DIM))), dtype=BF16)
    return pl.pallas_call(
        _c_in_kernel,
        grid=(t // tm,),
        in_specs=[row(D_MODEL)] + [_const_spec(a.shape) for a in (ng, w, ones_bd)] + [tab] * 4,
        out_specs=[row(wd) for wd, _ in outs],
        out_shape=[jax.ShapeDtypeStruct((t, wd), d) for wd, d in outs],
        compiler_params=_params("parallel"),
        name="c_in",
    )(x, ng, w, ones_bd, *tables)


def _attn_kernel(q_ref, k_ref, v_ref, o_ref, s_ref, *, tq):
    group = ATTN_HEADS // KV_HEADS
    nblk = q_ref.shape[1] // tq

    def rows_of(r):
        return pl.ds(pl.multiple_of(r * tq, tq), tq)

    def scores(r, g, buf):
        q = q_ref[0, rows_of(r), HEAD_DIM * g:HEAD_DIM * (g + 1)]
        s_ref[buf] = lax.dot_general(q, k_ref[0], _NT, preferred_element_type=F32)

    def finish(r, g, buf):
        sc = s_ref[buf]
        m = jnp.max(sc, axis=-1, keepdims=True)
        ov = _dot(jnp.exp2(sc - m).astype(BF16), v_ref[0])
        out = ov[:, :HEAD_DIM] / ov[:, HEAD_DIM:]
        o_ref[0, rows_of(r), HEAD_DIM * g:HEAD_DIM * (g + 1)] = out.astype(o_ref.dtype)

    scores(0, 0, 0)

    def body(r, carry):
        for g in range(group):
            if g + 1 < group:
                scores(r, g + 1, (g + 1) % 2)
            else:
                scores(jnp.minimum(r + 1, nblk - 1), 0, 0)
            finish(r, g, g % 2)
        return carry

    lax.fori_loop(0, nblk, body, 0, unroll=2)


def _attn(q, k, v, tq):
    b, s, _ = q.shape
    gw = ATTN_Q // KV_HEADS
    return pl.pallas_call(
        functools.partial(_attn_kernel, tq=tq),
        grid=(b, KV_HEADS),
        in_specs=[pl.BlockSpec((1, s, gw), lambda i, j: (i, 0, j)),
                  pl.BlockSpec((1, s, HEAD_DIM), lambda i, j: (i, 0, j)),
                  pl.BlockSpec((1, s, 2 * HEAD_DIM), lambda i, j: (i, 0, j))],
        out_specs=pl.BlockSpec((1, s, gw), lambda i, j: (i, 0, j)),
        out_shape=jax.ShapeDtypeStruct((b, s, ATTN_Q), BF16),
        scratch_shapes=[pltpu.VMEM((2, tq, s), F32)],
        compiler_params=_params("parallel", "parallel"),
        name="attn",
    )(q, k, v)


def _prep_weights(ab_norm, ab_w_in, gla_up_f, gla_bias_f, gla_up_b, gla_bias_b, gla_out_norm,
                  ab_w_out, c_norm, c_w_in, c_q_norm, c_k_norm, c_w_out, ffn_norm,
                  ffn_w_gate, ffn_w_up, ffn_w_down):
    n_main = 2 * GLA_QK + 2 * GLA_V
    w_in = ab_w_in[0]
    zeros = jnp.zeros((GLA_RANK, GLA_QK), F32)
    up = jnp.concatenate([jnp.concatenate([gla_up_f[0], zeros], axis=1),
                          jnp.concatenate([zeros, gla_up_b[0]], axis=1)], axis=0)
    qk_w = ATTN_Q + ATTN_KV
    c_w = c_w_in[0].astype(BF16)
    c_qk = c_w[:, :qk_w].reshape(D_MODEL, qk_w // HEAD_DIM, 2, 2, _QUARTER).swapaxes(2, 3)
    c_w = jnp.concatenate([c_qk.reshape(D_MODEL, qk_w), c_w[:, qk_w:]], axis=1)
    return dict(
        ab_ng=ab_norm[0][None, :],
        ab_wm=w_in[:, :n_main].astype(BF16),
        ab_wl=w_in[:, n_main:n_main + 2 * GLA_RANK].astype(BF16),
        ab_wf=w_in[:, n_main + 2 * GLA_RANK:].astype(BF16),
        ab_up=up,
        ab_bias=jnp.concatenate([gla_bias_f[0], gla_bias_b[0]])[None, :],
        ab_og=gla_out_norm[0][None, :],
        ab_wo=ab_w_out[0][:GLA_V].astype(BF16),
        ab_wfo=ab_w_out[0][GLA_V:].astype(BF16),
        c_ng=c_norm[0][None, :],
        c_w=c_w,
        c_qg=c_q_norm[0],
        c_kg=c_k_norm[0],
        c_wo=c_w_out[0].astype(BF16),
        ffn_ng=[ffn_norm[i][None, :] for i in range(2)],
        ffn_wg=[ffn_w_gate[i].astype(BF16) for i in range(2)],
        ffn_wu=[ffn_w_up[i].astype(BF16) for i in range(2)],
        ffn_wd=[ffn_w_down[i].astype(BF16) for i in range(2)],
    )


def _trunk(x, w, tm=512, tm_ffn=512, gla_blk=256, tq=256, fnet_groups=2):
    b, s, d = x.shape
    t = b * s
    xf = x.reshape(t, d)
    q, k, v, r, gf, gb, f = _ab_in(xf, w["ab_ng"], w["ab_wm"], w["ab_wl"], w["ab_wf"],
                                   w["ab_up"], w["ab_bias"], tm)
    seq = lambda a: a.reshape(b, s, a.shape[-1])
    o_f, o_b = _gla(seq(q), seq(k), seq(v), seq(gf), seq(gb), gla_blk)
    fo = _fnet(seq(f), fnet_groups)
    ffn = lambda i: (w["ffn_ng"][i], w["ffn_wg"][i], w["ffn_wu"][i], w["ffn_wd"][i])
    xf = _rows_call(_ab_out_ffn_kernel, "ab_out_ffn",
                    (o_f.reshape(t, GLA_V), o_b.reshape(t, GLA_V), r, fo.reshape(t, FNET_W), xf),
                    (w["ab_og"], w["ab_wo"], w["ab_wfo"]) + ffn(0), tm_ffn)
    q, k, v = _c_in(xf, w["c_ng"], w["c_w"], _rope_tables(s, w["c_qg"], w["c_kg"]), s, tm)
    o = _attn(seq(q), seq(k), seq(v), tq)
    xf = _rows_call(_c_out_ffn_kernel, "c_out_ffn", (o.reshape(t, ATTN_Q), xf), (w["c_wo"],) + ffn(1), tm_ffn)
    return xf.reshape(b, s, d)


def kernel(x_prompt, x_sample, ab_norm, ab_w_in, gla_up_f, gla_bias_f, gla_up_b, gla_bias_b,
           gla_out_norm, ab_w_out, c_norm, c_w_in, c_q_norm, c_k_norm, c_w_out, ffn_norm,
           ffn_w_gate, ffn_w_up, ffn_w_down):
    w = _prep_weights(ab_norm, ab_w_in, gla_up_f, gla_bias_f, gla_up_b, gla_bias_b, gla_out_norm,
                      ab_w_out, c_norm, c_w_in, c_q_norm, c_k_norm, c_w_out, ffn_norm,
                      ffn_w_gate, ffn_w_up, ffn_w_down)
    return _trunk(x_prompt, w), _trunk(x_sample, w)
```

```python
import functools

import numpy as np
import jax
import jax.numpy as jnp
from jax import lax
from jax.experimental import pallas as pl
from jax.experimental.pallas import tpu as pltpu

D_MODEL = 1024
GRID_W = 64
NORM_EPS = 1e-6
GLA_HEADS = 4
GLA_DK = 64
GLA_DV = 128
GLA_RANK = 16
GLA_GATE_NORM = 16.0
GLA_CHUNK = 64
FNET_GROUPS = 4
FNET_GW = 128
ATTN_HEADS = 8
KV_HEADS = 2
HEAD_DIM = 128
ROPE_THETA = 10000.0
FFN_HIDDEN = -(-8 * D_MODEL // (3 * 256)) * 256

GLA_QK = GLA_HEADS * GLA_DK
GLA_V = GLA_HEADS * GLA_DV
FNET_W = FNET_GROUPS * FNET_GW
ATTN_Q = ATTN_HEADS * HEAD_DIM
ATTN_KV = KV_HEADS * HEAD_DIM

SUBLANES = 8
VMEM_LIMIT = 56 * 1024 * 1024

BF16 = jnp.bfloat16
F32 = jnp.float32

_NT = (((1,), (1,)), ((), ()))
_TN = (((0,), (0,)), ((), ()))


def _params(*sem):
    return pltpu.CompilerParams(dimension_semantics=sem, vmem_limit_bytes=VMEM_LIMIT)


def _const_spec(shape):
    nd = len(shape)
    return pl.BlockSpec(shape, lambda *_: (0,) * nd, pipeline_mode=pl.Buffered(1))


def _dot(a, b):
    return jnp.dot(a, b, preferred_element_type=F32)


def _split_bf16(a):
    hi = a.astype(BF16)
    lo = (a - hi.astype(F32)).astype(BF16)
    return hi, lo


def _row_groups(ref, groups=2):
    n = ref.shape[0] // groups
    return [pl.ds(i * n, n) for i in range(groups)]


def _rms(x, gain):
    return x * lax.rsqrt(jnp.mean(x * x, axis=-1, keepdims=True) + NORM_EPS) * gain


def _ab_in_kernel(x_ref, ng_ref, wm_ref, wl_ref, wf_ref, up_ref, bias_ref,
                  q_ref, k_ref, v_ref, r_ref, gf_ref, gb_ref, f_ref):
    h = _rms(x_ref[...], ng_ref[...]).astype(BF16)
    low = _dot(h, wl_ref[...])
    y = _dot(h, wm_ref[...])
    q_ref[...] = (y[:, :GLA_QK] * (GLA_DK ** -0.5)).astype(BF16)
    k_ref[...] = y[:, GLA_QK:2 * GLA_QK].astype(BF16)
    v_ref[...] = y[:, 2 * GLA_QK:2 * GLA_QK + GLA_V].astype(BF16)
    r_ref[...] = y[:, 2 * GLA_QK + GLA_V:].astype(BF16)
    low_hi, low_lo = _split_bf16(low)
    up_hi, up_lo = _split_bf16(up_ref[...])
    z = _dot(low_hi, up_hi) + _dot(low_hi, up_lo) + _dot(low_lo, up_hi) + bias_ref[...]
    g = (jnp.minimum(z, 0.0) - jnp.log(1.0 + jnp.exp(-jnp.abs(z)))) * (1.0 / GLA_GATE_NORM)
    gf_ref[...] = g[:, :GLA_QK]
    gb_ref[...] = g[:, GLA_QK:]
    f_ref[...] = _dot(h, wf_ref[...]).astype(BF16)


def _ab_in(x, ng, wm, wl, wf, up, bias, tm):
    t = x.shape[0]
    row = lambda w: pl.BlockSpec((tm, w), lambda i: (i, 0))
    outs = [(GLA_QK, BF16), (GLA_QK, BF16), (GLA_V, BF16), (GLA_V, BF16),
            (GLA_QK, F32), (GLA_QK, F32), (FNET_W, BF16)]
    return pl.pallas_call(
        _ab_in_kernel,
        grid=(t // tm,),
        in_specs=[row(D_MODEL)] + [_const_spec(a.shape) for a in (ng, wm, wl, wf, up, bias)],
        out_specs=[row(w) for w, _ in outs],
        out_shape=[jax.ShapeDtypeStruct((t, w), d) for w, d in outs],
        compiler_params=_params("parallel"),
        name="ab_in",
    )(x, ng, wm, wl, wf, up, bias)


def _gla_local(q_ref, k_ref, v_ref, g_ref, *, chunks, reverse):
    c = GLA_CHUNK
    n = chunks * c
    pair_w = 2 * GLA_DK
    ri = lax.broadcasted_iota(jnp.int32, (n, n), 0)
    ci = lax.broadcasted_iota(jnp.int32, (n, n), 1)
    causal = (ri <= ci) if reverse else (ri >= ci)
    tri = jnp.where(causal & ((ri // c) == (ci // c)), 1.0, 0.0).astype(BF16)
    g_hi, g_lo = _split_bf16(g_ref[0])
    bc = _dot(tri, g_hi) + _dot(tri, g_lo)
    q_in = (q_ref[0].astype(F32) * jnp.exp(bc)).astype(BF16)
    k_in = k_ref[0].astype(F32) * jnp.exp(-bc)
    v = v_ref[0]
    last = [j * c if reverse else (j + 1) * c - 1 for j in range(chunks)]
    tot = jnp.concatenate([bc[t:t + 1, :] for t in last] + [jnp.zeros((pair_w - chunks, GLA_QK), F32)],
                          axis=0)
    e_row = jnp.exp(tot)
    e_col = jnp.exp(tot.T)
    r2 = lax.broadcasted_iota(jnp.int32, (c, pair_w), 0)
    c2 = lax.broadcasted_iota(jnp.int32, (c, pair_w), 1) % c
    keep = (r2 <= c2) if reverse else (r2 >= c2)
    first = lax.broadcasted_iota(jnp.int32, (1, pair_w), 1) < GLA_DK
    rb = lax.broadcasted_iota(jnp.int32, (pair_w, 2 * GLA_DV), 0) < GLA_DK
    cb = lax.broadcasted_iota(jnp.int32, (pair_w, 2 * GLA_DV), 1) < GLA_DV
    diag = rb == cb
    local = {}
    for j in range(chunks):
        rows = slice(j * c, (j + 1) * c)
        k_c = k_in[rows]
        k_dec = (k_c * e_row[j:j + 1, :]).astype(BF16)
        for p in range(GLA_HEADS // 2):
            pair = slice(pair_w * p, pair_w * (p + 1))
            kp = k_c[:, pair]
            k_bd = jnp.concatenate([jnp.where(first, kp, 0.0), jnp.where(first, 0.0, kp)], axis=0).astype(BF16)
            qp = q_in[rows, pair]
            att = lax.dot_general(qp, k_bd, _NT, preferred_element_type=F32)
            att = jnp.where(keep, att, 0.0).astype(BF16)
            vp = v[rows, 2 * GLA_DV * p:2 * GLA_DV * (p + 1)]
            v_bd = jnp.where(diag, jnp.concatenate([vp, vp], axis=0), jnp.zeros((), BF16))
            upd = lax.dot_general(k_dec[:, pair], vp, _TN, preferred_element_type=F32)
            local[j, p] = (jnp.concatenate([att, qp], axis=1), v_bd, jnp.where(diag, upd, 0.0),
                           e_col[pair, j:j + 1])
    return local


def _gla_serial(local, o_ref, state_ref, *, chunks, reverse):
    c = GLA_CHUNK
    states = [state_ref[p] for p in range(GLA_HEADS // 2)]
    for j in (reversed(range(chunks)) if reverse else range(chunks)):
        outs = []
        for p in range(GLA_HEADS // 2):
            lhs, v_bd, upd, decay = local[j, p]
            outs.append(_dot(lhs, jnp.concatenate([v_bd, states[p].astype(BF16)], axis=0)))
            states[p] = states[p] * decay + upd
        o_ref[0, j * c:(j + 1) * c, :] = jnp.concatenate(outs, axis=-1)
    for p in range(GLA_HEADS // 2):
        state_ref[p] = states[p]


def _gla_kernel(qf_ref, kf_ref, vf_ref, gf_ref, qb_ref, kb_ref, vb_ref, gb_ref,
                of_ref, ob_ref, sf_ref, sb_ref, *, chunks):
    @pl.when(pl.program_id(1) == 0)
    def _():
        sf_ref[...] = jnp.zeros_like(sf_ref)
        sb_ref[...] = jnp.zeros_like(sb_ref)

    fwd = _gla_local(qf_ref, kf_ref, vf_ref, gf_ref, chunks=chunks, reverse=False)
    bwd = _gla_local(qb_ref, kb_ref, vb_ref, gb_ref, chunks=chunks, reverse=True)
    _gla_serial(fwd, of_ref, sf_ref, chunks=chunks, reverse=False)
    _gla_serial(bwd, ob_ref, sb_ref, chunks=chunks, reverse=True)


def _gla(q, k, v, gf, gb, blk):
    b, s, _ = q.shape
    nb = s // blk
    fwd = lambda w: pl.BlockSpec((1, blk, w), lambda i, n: (i, n, 0))
    bwd = lambda w: pl.BlockSpec((1, blk, w), lambda i, n: (i, nb - 1 - n, 0))
    state = pltpu.VMEM((GLA_HEADS // 2, 2 * GLA_DK, 2 * GLA_DV), F32)
    return pl.pallas_call(
        functools.partial(_gla_kernel, chunks=blk // GLA_CHUNK),
        grid=(b, nb),
        in_specs=[fwd(GLA_QK), fwd(GLA_QK), fwd(GLA_V), fwd(GLA_QK),
                  bwd(GLA_QK), bwd(GLA_QK), bwd(GLA_V), bwd(GLA_QK)],
        out_specs=[fwd(GLA_V), bwd(GLA_V)],
        out_shape=[jax.ShapeDtypeStruct((b, s, GLA_V), F32)] * 2,
        scratch_shapes=[state, state],
        compiler_params=_params("parallel", "arbitrary"),
        name="gla",
    )(q, k, v, gf, q, k, v, gb)


def _fnet_kernel(x_ref, ww_ref, k1c_ref, k1s_ref, k2c_ref, k2s_ref, twc_ref, tws_ref,
                 o_ref, z_ref, a_ref, *, nl, nm, groups):
    gw = FNET_GW
    cw = groups * gw
    rows_per = SUBLANES * nm
    for i in range(nl // SUBLANES):
        xs = x_ref[0, pl.ds(i * rows_per, rows_per), :]
        zr, zi = [], []
        for g in range(groups):
            zz = _dot(xs[:, g * gw:(g + 1) * gw], ww_ref[...])
            zr.append(zz[:, :gw])
            zi.append(zz[:, gw:])
        zcat = jnp.concatenate(zr + zi, axis=-1)
        z_ref[pl.ds(i * SUBLANES, SUBLANES)] = zcat.reshape(SUBLANES, nm // SUBLANES, SUBLANES, 2 * cw)
    for mb in range(nm // SUBLANES):
        z = z_ref[:, mb].reshape(nl * SUBLANES, 2 * cw)
        zr, zi = z[:, :cw], z[:, cw:]
        za = z.astype(BF16)
        zb = jnp.concatenate([zi, -zr], axis=-1).astype(BF16)
        a = _dot(k1c_ref[...], za) + _dot(k1s_ref[...], zb)
        ar, ai = a[:, :cw], a[:, cw:]
        tc = twc_ref[:, mb].reshape(nl * SUBLANES, gw)
        ts = tws_ref[:, mb].reshape(nl * SUBLANES, gw)
        tc = jnp.concatenate([tc] * groups, axis=-1)
        ts = jnp.concatenate([ts] * groups, axis=-1)
        pr = ar * tc + ai * ts
        pi = ai * tc - ar * ts
        a_ref[:, mb] = jnp.concatenate([pr, pi], axis=-1).reshape(nl, SUBLANES, 2 * cw)
    for ab in range(nl // SUBLANES):
        bb = a_ref[pl.ds(ab * SUBLANES, SUBLANES)].reshape(SUBLANES * nm, 2 * cw).astype(BF16)
        res = _dot(k2c_ref[...], bb[:, :cw]) + _dot(k2s_ref[...], bb[:, cw:])
        o_ref[0, :, ab] = res.reshape(nm, SUBLANES, cw)


def _dft_tables(s):
    nm = 64
    nl = s // nm
    assert nl * nm == s and nl % SUBLANES == 0
    eye = np.eye(SUBLANES)

    def dft(n):
        idx = np.arange(n)
        ang = 2.0 * np.pi * ((idx[:, None] * idx[None, :]) % n) / n
        return np.cos(ang), np.sin(ang)

    k1c, k1s = (np.kron(f, eye) for f in dft(nl))
    k2c, k2s = (np.einsum("nm,ij->nijm", f, eye).reshape(nm * SUBLANES, SUBLANES * nm)
                for f in dft(nm))
    a = np.arange(nl)[:, None]
    m = np.arange(nm)[None, :]
    ang = 2.0 * np.pi * ((a * m) % s) / s
    norm = 1.0 / np.sqrt(float(s) * FNET_GW)
    shape = (nl, nm // SUBLANES, SUBLANES, FNET_GW)
    twc = np.broadcast_to((np.cos(ang) * norm).reshape(nl, nm // SUBLANES, SUBLANES, 1), shape)
    tws = np.broadcast_to((np.sin(ang) * norm).reshape(nl, nm // SUBLANES, SUBLANES, 1), shape)
    w = np.arange(FNET_GW)
    angw = 2.0 * np.pi * ((w[:, None] * w[None, :]) % FNET_GW) / FNET_GW
    ww = np.concatenate([np.cos(angw), -np.sin(angw)], axis=1)
    bf = lambda v: jnp.asarray(v, dtype=F32).astype(BF16)
    return (nl, nm, bf(ww), bf(k1c), bf(k1s), bf(k2c), bf(k2s),
            jnp.asarray(twc, dtype=F32), jnp.asarray(tws, dtype=F32))


def _fnet(f, groups):
    b, s, _ = f.shape
    nl, nm, ww, k1c, k1s, k2c, k2s, twc, tws = _dft_tables(s)
    cw = groups * FNET_GW
    consts = (ww, k1c, k1s, k2c, k2s, twc, tws)
    out = pl.pallas_call(
        functools.partial(_fnet_kernel, nl=nl, nm=nm, groups=groups),
        grid=(b, FNET_W // cw),
        in_specs=[pl.BlockSpec((1, s, cw), lambda i, j: (i, 0, j))]
                 + [_const_spec(a.shape) for a in consts],
        out_specs=pl.BlockSpec((1, nm, nl // SUBLANES, SUBLANES, cw), lambda i, j: (i, 0, 0, 0, j)),
        out_shape=jax.ShapeDtypeStruct((b, nm, nl // SUBLANES, SUBLANES, FNET_W), F32),
        scratch_shapes=[pltpu.VMEM((nl, nm // SUBLANES, SUBLANES, 2 * cw), F32),
                        pltpu.VMEM((nl, nm // SUBLANES, SUBLANES, 2 * cw), F32)],
        compiler_params=_params("parallel", "parallel"),
        name="fnet",
    )(f, *consts)
    return out.reshape(b, s, FNET_W)


FFN_SPLIT = 6 * 256


def _swiglu(xs, ng_ref, wg_ref, wu_ref, wd_ref):
    hs = [_rms(x, ng_ref[...]).astype(BF16) for x in xs]
    ys = list(xs)
    for cols in (slice(0, FFN_SPLIT), slice(FFN_SPLIT, FFN_HIDDEN)):
        gu = [(_dot(h, wg_ref[:, cols]), _dot(h, wu_ref[:, cols])) for h in hs]
        acts = [(g * jax.nn.sigmoid(g) * u).astype(BF16) for g, u in gu]
        ys = [y + _dot(a, wd_ref[cols, :]) for y, a in zip(ys, acts)]
    return ys


def _ab_out_ffn_kernel(of_ref, ob_ref, r_ref, fo_ref, x_ref, og_ref, wo_ref, wf_ref,
                       ng_ref, wg_ref, wu_ref, wd_ref, y_ref):
    groups = _row_groups(y_ref)
    xs = []
    for rows in groups:
        o = of_ref[rows, :] + ob_ref[rows, :]
        heads = [_rms(o[:, GLA_DV * h:GLA_DV * (h + 1)], og_ref[...]) for h in range(GLA_HEADS)]
        r = r_ref[rows, :].astype(F32)
        o = jnp.concatenate(heads, axis=-1) * (r * jax.nn.sigmoid(r))
        mixed = _dot(o.astype(BF16), wo_ref[...]) + _dot(fo_ref[rows, :].astype(BF16), wf_ref[...])
        xs.append(x_ref[rows, :] + mixed)
    for rows, y in zip(groups, _swiglu(xs, ng_ref, wg_ref, wu_ref, wd_ref)):
        y_ref[rows, :] = y


def _c_out_ffn_kernel(o_ref, x_ref, w_ref, ng_ref, wg_ref, wu_ref, wd_ref, y_ref):
    groups = _row_groups(y_ref)
    xs = [x_ref[rows, :] + _dot(o_ref[rows, :], w_ref[...]) for rows in groups]
    for rows, y in zip(groups, _swiglu(xs, ng_ref, wg_ref, wu_ref, wd_ref)):
        y_ref[rows, :] = y


def _rows_call(body, name, row_args, const_args, tm):
    t = row_args[0].shape[0]
    row = lambda a: pl.BlockSpec((tm, a.shape[1]), lambda i: (i, 0))
    return pl.pallas_call(
        body,
        grid=(t // tm,),
        in_specs=[row(a) for a in row_args] + [_const_spec(a.shape) for a in const_args],
        out_specs=pl.BlockSpec((tm, D_MODEL), lambda i: (i, 0)),
        out_shape=jax.ShapeDtypeStruct((t, D_MODEL), F32),
        compiler_params=_params("parallel"),
        name=name,
    )(*row_args, *const_args)


_QUARTER = HEAD_DIM // 4
_HEAD_PERM = np.concatenate([np.arange(0, _QUARTER), np.arange(2 * _QUARTER, 3 * _QUARTER),
                             np.arange(_QUARTER, 2 * _QUARTER), np.arange(3 * _QUARTER, 4 * _QUARTER)])


def _qk_norm_rope(y2, ones_bd, gain, cos, sin):
    sq = (y2 * y2).astype(BF16)
    ms = _dot(sq, ones_bd) * (1.0 / HEAD_DIM)
    inv = lax.rsqrt(ms + NORM_EPS)
    halves = []
    for j in range(2):
        yh = y2[:, HEAD_DIM * j:HEAD_DIM * (j + 1)] * gain
        halves.append(yh * cos + pltpu.roll(yh, HEAD_DIM // 2, 1) * sin)
    return jnp.concatenate(halves, axis=-1) * inv


def _c_in_kernel(x_ref, ng_ref, w_ref, bd_ref, qg_ref, kg_ref, cos_ref, sin_ref, q_ref, k_ref, v_ref):
    h = _rms(x_ref[...], ng_ref[...]).astype(BF16)
    y = _dot(h, w_ref[...])
    ones_bd = bd_ref[...]
    cos, sin = cos_ref[...], sin_ref[...]
    for i in range(0, ATTN_HEADS, 2):
        cols = slice(HEAD_DIM * i, HEAD_DIM * (i + 2))
        q_ref[:, cols] = _qk_norm_rope(y[:, cols], ones_bd, qg_ref[...], cos, sin).astype(BF16)
    k_ref[...] = _qk_norm_rope(y[:, ATTN_Q:ATTN_Q + ATTN_KV], ones_bd, kg_ref[...], cos, sin).astype(BF16)
    v_ref[...] = y[:, ATTN_Q + ATTN_KV:].astype(BF16)


def _rope_tables(s):
    pos = np.arange(s)
    half = HEAD_DIM // 2
    inv = ROPE_THETA ** (-np.arange(0, half, 2, dtype=np.float64) / half)
    ar = (pos // GRID_W)[:, None] * inv[None, :]
    ac = (pos % GRID_W)[:, None] * inv[None, :]
    ang = np.concatenate([ar, ac, ar, ac], axis=-1)
    sign = np.where(np.arange(HEAD_DIM) < half, -1.0, 1.0)[None, :]
    return jnp.asarray(np.cos(ang), dtype=F32), jnp.asarray(np.sin(ang) * sign, dtype=F32)


def _c_in(x, ng, w, qg, kg, s, tm):
    t = x.shape[0]
    per_seq = s // tm
    row = lambda wd: pl.BlockSpec((tm, wd), lambda i: (i, 0))
    tab = pl.BlockSpec((tm, HEAD_DIM), lambda i: (i % per_seq, 0))
    outs = [(ATTN_Q, BF16), (ATTN_KV, BF16), (ATTN_KV, BF16)]
    ones_bd = jnp.asarray(np.kron(np.eye(2), np.ones((HEAD_DIM, HEAD_DIM))), dtype=BF16)
    consts = (ng, w, ones_bd, qg, kg)
    return pl.pallas_call(
        _c_in_kernel,
        grid=(t // tm,),
        in_specs=[row(D_MODEL)] + [_const_spec(a.shape) for a in consts] + [tab] * 2,
        out_specs=[row(wd) for wd, _ in outs],
        out_shape=[jax.ShapeDtypeStruct((t, wd), d) for wd, d in outs],
        compiler_params=_params("parallel"),
        name="c_in",
    )(x, *consts, *_rope_tables(s))


SUM_ROWS = 16


def _attn_kernel(q_ref, k_ref, vt_ref, o_ref, s_ref, m_ref, *, tq):
    group = ATTN_HEADS // KV_HEADS
    nblk = q_ref.shape[1] // tq

    def rows_of(r):
        return pl.ds(pl.multiple_of(r * tq, tq), tq)

    def scores(r, g):
        q = q_ref[0, rows_of(r), HEAD_DIM * g:HEAD_DIM * (g + 1)]
        s_ref[g] = lax.dot_general(k_ref[0], q, _NT, preferred_element_type=F32)

    def col_max(g):
        m_ref[g] = jnp.max(s_ref[g], axis=0, keepdims=True)

    def finish(r, g):
        ov = _dot(vt_ref[0, 0], jnp.exp2(s_ref[g] - m_ref[g]).astype(BF16))
        out = ov[:HEAD_DIM] / ov[HEAD_DIM:HEAD_DIM + 1]
        o_ref[0, rows_of(r), HEAD_DIM * g:HEAD_DIM * (g + 1)] = out.T.astype(o_ref.dtype)

    scores(0, 0)
    scores(0, 1)
    col_max(0)

    def body(r, carry):
        nxt = jnp.minimum(r + 1, nblk - 1)
        for g in range(group):
            scores(r if g + 2 < group else nxt, (g + 2) % group)
            col_max((g + 1) % group)
            finish(r, g)
        return carry

    lax.fori_loop(0, nblk, body, 0, unroll=4)


def _attn(q, k, v, tq):
    b, s, _ = q.shape
    gw = ATTN_Q // KV_HEADS
    vt = jnp.transpose(v.reshape(b, s, KV_HEADS, HEAD_DIM), (0, 2, 3, 1))
    vt = jnp.concatenate([vt, jnp.ones((b, KV_HEADS, SUM_ROWS, s), BF16)], axis=2)
    return pl.pallas_call(
        functools.partial(_attn_kernel, tq=tq),
        grid=(b, KV_HEADS),
        in_specs=[pl.BlockSpec((1, s, gw), lambda i, j: (i, 0, j)),
                  pl.BlockSpec((1, s, HEAD_DIM), lambda i, j: (i, 0, j)),
                  pl.BlockSpec((1, 1, HEAD_DIM + SUM_ROWS, s), lambda i, j: (i, j, 0, 0))],
        out_specs=pl.BlockSpec((1, s, gw), lambda i, j: (i, 0, j)),
        out_shape=jax.ShapeDtypeStruct((b, s, ATTN_Q), BF16),
        scratch_shapes=[pltpu.VMEM((ATTN_HEADS // KV_HEADS, s, tq), F32),
                        pltpu.VMEM((ATTN_HEADS // KV_HEADS, 1, tq), F32)],
        compiler_params=_params("parallel", "parallel"),
        name="attn",
    )(q, k, vt)


def _prep_weights(ab_norm, ab_w_in, gla_up_f, gla_bias_f, gla_up_b, gla_bias_b, gla_out_norm,
                  ab_w_out, c_norm, c_w_in, c_q_norm, c_k_norm, c_w_out, ffn_norm,
                  ffn_w_gate, ffn_w_up, ffn_w_down):
    n_main = 2 * GLA_QK + 2 * GLA_V
    w_in = ab_w_in[0]
    zeros = jnp.zeros((GLA_RANK, GLA_QK), F32)
    up = jnp.concatenate([jnp.concatenate([gla_up_f[0], zeros], axis=1),
                          jnp.concatenate([zeros, gla_up_b[0]], axis=1)], axis=0)
    qk_w = ATTN_Q + ATTN_KV
    c_w = c_w_in[0].astype(BF16)
    c_qk = c_w[:, :qk_w].reshape(D_MODEL, qk_w // HEAD_DIM, 2, 2, _QUARTER).swapaxes(2, 3)
    c_w = jnp.concatenate([c_qk.reshape(D_MODEL, qk_w), c_w[:, qk_w:]], axis=1)
    return dict(
        ab_ng=ab_norm[0][None, :],
        ab_wm=w_in[:, :n_main].astype(BF16),
        ab_wl=w_in[:, n_main:n_main + 2 * GLA_RANK].astype(BF16),
        ab_wf=w_in[:, n_main + 2 * GLA_RANK:].astype(BF16),
        ab_up=up,
        ab_bias=jnp.concatenate([gla_bias_f[0], gla_bias_b[0]])[None, :],
        ab_og=gla_out_norm[0][None, :],
        ab_wo=ab_w_out[0][:GLA_V].astype(BF16),
        ab_wfo=ab_w_out[0][GLA_V:].astype(BF16),
        c_ng=c_norm[0][None, :],
        c_w=c_w,
        c_qg=(c_q_norm[0][_HEAD_PERM] * ((HEAD_DIM ** -0.5) * float(np.log2(np.e))))[None, :],
        c_kg=c_k_norm[0][_HEAD_PERM][None, :],
        c_wo=c_w_out[0].astype(BF16),
        ffn_ng=[ffn_norm[i][None, :] for i in range(2)],
        ffn_wg=[ffn_w_gate[i].astype(BF16) for i in range(2)],
        ffn_wu=[ffn_w_up[i].astype(BF16) for i in range(2)],
        ffn_wd=[ffn_w_down[i].astype(BF16) for i in range(2)],
    )


def _trunk(x, w, tm=512, tm_ffn=512, gla_blk=256, tq=256, fnet_groups=2):
    b, s, d = x.shape
    t = b * s
    xf = x.reshape(t, d)
    q, k, v, r, gf, gb, f = _ab_in(xf, w["ab_ng"], w["ab_wm"], w["ab_wl"], w["ab_wf"],
                                   w["ab_up"], w["ab_bias"], tm)
    seq = lambda a: a.reshape(b, s, a.shape[-1])
    o_f, o_b = _gla(seq(q), seq(k), seq(v), seq(gf), seq(gb), gla_blk)
    fo = _fnet(seq(f), fnet_groups)
    ffn = lambda i: (w["ffn_ng"][i], w["ffn_wg"][i], w["ffn_wu"][i], w["ffn_wd"][i])
    xf = _rows_call(_ab_out_ffn_kernel, "ab_out_ffn",
                    (o_f.reshape(t, GLA_V), o_b.reshape(t, GLA_V), r, fo.reshape(t, FNET_W), xf),
                    (w["ab_og"], w["ab_wo"], w["ab_wfo"]) + ffn(0), tm_ffn)
    q, k, v = _c_in(xf, w["c_ng"], w["c_w"], w["c_qg"], w["c_kg"], s, tm)
    o = _attn(seq(q), seq(k), seq(v), tq)
    xf = _rows_call(_c_out_ffn_kernel, "c_out_ffn", (o.reshape(t, ATTN_Q), xf), (w["c_wo"],) + ffn(1), tm_ffn)
    return xf.reshape(b, s, d)


def kernel(x_prompt, x_sample, ab_norm, ab_w_in, gla_up_f, gla_bias_f, gla_up_b, gla_bias_b,
           gla_out_norm, ab_w_out, c_norm, c_w_in, c_q_norm, c_k_norm, c_w_out, ffn_norm,
           ffn_w_gate, ffn_w_up, ffn_w_down):
    w = _prep_weights(ab_norm, ab_w_in, gla_up_f, gla_bias_f, gla_up_b, gla_bias_b, gla_out_norm,
                      ab_w_out, c_norm, c_w_in, c_q_norm, c_k_norm, c_w_out, ffn_norm,
                      ffn_w_gate, ffn_w_up, ffn_w_down)
    return _trunk(x_prompt, w), _trunk(x_sample, w)
```

```python
import functools

import numpy as np
import jax
import jax.numpy as jnp
from jax import lax
from jax.experimental import pallas as pl
from jax.experimental.pallas import tpu as pltpu

D_MODEL = 1024
GRID_W = 64
NORM_EPS = 1e-6
GLA_HEADS = 4
GLA_DK = 64
GLA_DV = 128
GLA_RANK = 16
GLA_GATE_NORM = 16.0
GLA_CHUNK = 64
FNET_GROUPS = 4
FNET_GW = 128
ATTN_HEADS = 8
KV_HEADS = 2
HEAD_DIM = 128
ROPE_THETA = 10000.0
FFN_HIDDEN = -(-8 * D_MODEL // (3 * 256)) * 256

GLA_QK = GLA_HEADS * GLA_DK
GLA_V = GLA_HEADS * GLA_DV
FNET_W = FNET_GROUPS * FNET_GW
ATTN_Q = ATTN_HEADS * HEAD_DIM
ATTN_KV = KV_HEADS * HEAD_DIM

SUBLANES = 8
VMEM_LIMIT = 56 * 1024 * 1024

BF16 = jnp.bfloat16
F32 = jnp.float32

_NT = (((1,), (1,)), ((), ()))
_TN = (((0,), (0,)), ((), ()))


def _params(*sem):
    return pltpu.CompilerParams(dimension_semantics=sem, vmem_limit_bytes=VMEM_LIMIT)


def _const_spec(shape):
    nd = len(shape)
    return pl.BlockSpec(shape, lambda *_: (0,) * nd, pipeline_mode=pl.Buffered(1))


def _dot(a, b):
    return jnp.dot(a, b, preferred_element_type=F32)


def _split_bf16(a):
    hi = a.astype(BF16)
    lo = (a - hi.astype(F32)).astype(BF16)
    return hi, lo


def _row_groups(ref, groups=2):
    n = ref.shape[0] // groups
    return [pl.ds(i * n, n) for i in range(groups)]


def _rms(x, gain):
    return x * lax.rsqrt(jnp.mean(x * x, axis=-1, keepdims=True) + NORM_EPS) * gain


def _ab_in_kernel(x_ref, ng_ref, wm_ref, wl_ref, wf_ref, up_ref, bias_ref,
                  q_ref, k_ref, v_ref, r_ref, gf_ref, gb_ref, f_ref):
    h = _rms(x_ref[...], ng_ref[...]).astype(BF16)
    low = _dot(h, wl_ref[...])
    y = _dot(h, wm_ref[...])
    q_ref[...] = (y[:, :GLA_QK] * (GLA_DK ** -0.5)).astype(BF16)
    k_ref[...] = y[:, GLA_QK:2 * GLA_QK].astype(BF16)
    v_ref[...] = y[:, 2 * GLA_QK:2 * GLA_QK + GLA_V].astype(BF16)
    r_ref[...] = y[:, 2 * GLA_QK + GLA_V:].astype(BF16)
    low_hi, low_lo = _split_bf16(low)
    up_hi, up_lo = _split_bf16(up_ref[...])
    z = _dot(low_hi, up_hi) + _dot(low_hi, up_lo) + _dot(low_lo, up_hi) + bias_ref[...]
    g = (jnp.minimum(z, 0.0) - jnp.log(1.0 + jnp.exp(-jnp.abs(z)))) * (1.0 / GLA_GATE_NORM)
    gf_ref[...] = g[:, :GLA_QK]
    gb_ref[...] = g[:, GLA_QK:]
    f_ref[...] = _dot(h, wf_ref[...]).astype(BF16)


def _ab_in(x, ng, wm, wl, wf, up, bias, tm):
    t = x.shape[0]
    row = lambda w: pl.BlockSpec((tm, w), lambda i: (i, 0))
    outs = [(GLA_QK, BF16), (GLA_QK, BF16), (GLA_V, BF16), (GLA_V, BF16),
            (GLA_QK, F32), (GLA_QK, F32), (FNET_W, BF16)]
    return pl.pallas_call(
        _ab_in_kernel,
        grid=(t // tm,),
        in_specs=[row(D_MODEL)] + [_const_spec(a.shape) for a in (ng, wm, wl, wf, up, bias)],
        out_specs=[row(w) for w, _ in outs],
        out_shape=[jax.ShapeDtypeStruct((t, w), d) for w, d in outs],
        compiler_params=_params("parallel"),
        name="ab_in",
    )(x, ng, wm, wl, wf, up, bias)


def _gla_local(q_ref, k_ref, v_ref, g_ref, *, chunks, reverse):
    c = GLA_CHUNK
    n = chunks * c
    pair_w = 2 * GLA_DK
    ri = lax.broadcasted_iota(jnp.int32, (n, n), 0)
    ci = lax.broadcasted_iota(jnp.int32, (n, n), 1)
    causal = (ri <= ci) if reverse else (ri >= ci)
    tri = jnp.where(causal & ((ri // c) == (ci // c)), 1.0, 0.0).astype(BF16)
    g_hi, g_lo = _split_bf16(g_ref[0])
    bc = _dot(tri, g_hi) + _dot(tri, g_lo)
    q_in = (q_ref[0].astype(F32) * jnp.exp(bc)).astype(BF16)
    k_in = k_ref[0].astype(F32) * jnp.exp(-bc)
    v = v_ref[0]
    last = [j * c if reverse else (j + 1) * c - 1 for j in range(chunks)]
    tot = jnp.concatenate([bc[t:t + 1, :] for t in last] + [jnp.zeros((pair_w - chunks, GLA_QK), F32)],
                          axis=0)
    e_row = jnp.exp(tot)
    e_col = jnp.exp(tot.T)
    r2 = lax.broadcasted_iota(jnp.int32, (c, pair_w), 0)
    c2 = lax.broadcasted_iota(jnp.int32, (c, pair_w), 1) % c
    keep = (r2 <= c2) if reverse else (r2 >= c2)
    first = lax.broadcasted_iota(jnp.int32, (1, pair_w), 1) < GLA_DK
    rb = lax.broadcasted_iota(jnp.int32, (pair_w, 2 * GLA_DV), 0) < GLA_DK
    cb = lax.broadcasted_iota(jnp.int32, (pair_w, 2 * GLA_DV), 1) < GLA_DV
    diag = rb == cb
    local = {}
    for j in range(chunks):
        rows = slice(j * c, (j + 1) * c)
        k_c = k_in[rows]
        k_dec = (k_c * e_row[j:j + 1, :]).astype(BF16)
        for p in range(GLA_HEADS // 2):
            pair = slice(pair_w * p, pair_w * (p + 1))
            kp = k_c[:, pair]
            k_bd = jnp.concatenate([jnp.where(first, kp, 0.0), jnp.where(first, 0.0, kp)], axis=0).astype(BF16)
            qp = q_in[rows, pair]
            att = lax.dot_general(qp, k_bd, _NT, preferred_element_type=F32)
            att = jnp.where(keep, att, 0.0).astype(BF16)
            vp = v[rows, 2 * GLA_DV * p:2 * GLA_DV * (p + 1)]
            v_bd = jnp.where(diag, jnp.concatenate([vp, vp], axis=0), jnp.zeros((), BF16))
            upd = lax.dot_general(k_dec[:, pair], vp, _TN, preferred_element_type=F32)
            local[j, p] = (jnp.concatenate([att, qp], axis=1), v_bd, jnp.where(diag, upd, 0.0),
                           e_col[pair, j:j + 1])
    return local


def _gla_serial(local, o_ref, state_ref, *, chunks, reverse):
    c = GLA_CHUNK
    states = [state_ref[p] for p in range(GLA_HEADS // 2)]
    for j in (reversed(range(chunks)) if reverse else range(chunks)):
        outs = []
        for p in range(GLA_HEADS // 2):
            lhs, v_bd, upd, decay = local[j, p]
            outs.append(_dot(lhs, jnp.concatenate([v_bd, states[p].astype(BF16)], axis=0)))
            states[p] = states[p] * decay + upd
        o_ref[0, j * c:(j + 1) * c, :] = jnp.concatenate(outs, axis=-1)
    for p in range(GLA_HEADS // 2):
        state_ref[p] = states[p]


def _gla_kernel(qf_ref, kf_ref, vf_ref, gf_ref, qb_ref, kb_ref, vb_ref, gb_ref,
                of_ref, ob_ref, sf_ref, sb_ref, *, chunks):
    @pl.when(pl.program_id(1) == 0)
    def _():
        sf_ref[...] = jnp.zeros_like(sf_ref)
        sb_ref[...] = jnp.zeros_like(sb_ref)

    fwd = _gla_local(qf_ref, kf_ref, vf_ref, gf_ref, chunks=chunks, reverse=False)
    bwd = _gla_local(qb_ref, kb_ref, vb_ref, gb_ref, chunks=chunks, reverse=True)
    _gla_serial(fwd, of_ref, sf_ref, chunks=chunks, reverse=False)
    _gla_serial(bwd, ob_ref, sb_ref, chunks=chunks, reverse=True)


def _gla(q, k, v, gf, gb, blk):
    b, s, _ = q.shape
    nb = s // blk
    fwd = lambda w: pl.BlockSpec((1, blk, w), lambda i, n: (i, n, 0))
    bwd = lambda w: pl.BlockSpec((1, blk, w), lambda i, n: (i, nb - 1 - n, 0))
    state = pltpu.VMEM((GLA_HEADS // 2, 2 * GLA_DK, 2 * GLA_DV), F32)
    return pl.pallas_call(
        functools.partial(_gla_kernel, chunks=blk // GLA_CHUNK),
        grid=(b, nb),
        in_specs=[fwd(GLA_QK), fwd(GLA_QK), fwd(GLA_V), fwd(GLA_QK),
                  bwd(GLA_QK), bwd(GLA_QK), bwd(GLA_V), bwd(GLA_QK)],
        out_specs=[fwd(GLA_V), bwd(GLA_V)],
        out_shape=[jax.ShapeDtypeStruct((b, s, GLA_V), F32)] * 2,
        scratch_shapes=[state, state],
        compiler_params=_params("parallel", "arbitrary"),
        name="gla",
    )(q, k, v, gf, q, k, v, gb)


def _fnet_kernel(x_ref, ww_ref, k1c_ref, k1s_ref, k2c_ref, k2s_ref, twc_ref, tws_ref,
                 o_ref, z_ref, a_ref, *, nl, nm, groups):
    gw = FNET_GW
    cw = groups * gw
    rows_per = SUBLANES * nm
    for i in range(nl // SUBLANES):
        xs = x_ref[0, pl.ds(i * rows_per, rows_per), :]
        zr, zi = [], []
        for g in range(groups):
            zz = _dot(xs[:, g * gw:(g + 1) * gw], ww_ref[...])
            zr.append(zz[:, :gw])
            zi.append(zz[:, gw:])
        zcat = jnp.concatenate(zr + zi, axis=-1)
        z_ref[pl.ds(i * SUBLANES, SUBLANES)] = zcat.reshape(SUBLANES, nm // SUBLANES, SUBLANES, 2 * cw)
    for mb in range(nm // SUBLANES):
        z = z_ref[:, mb].reshape(nl * SUBLANES, 2 * cw)
        zr, zi = z[:, :cw], z[:, cw:]
        za = z.astype(BF16)
        zb = jnp.concatenate([zi, -zr], axis=-1).astype(BF16)
        a = _dot(k1c_ref[...], za) + _dot(k1s_ref[...], zb)
        ar, ai = a[:, :cw], a[:, cw:]
        tc = twc_ref[:, mb].reshape(nl * SUBLANES, gw)
        ts = tws_ref[:, mb].reshape(nl * SUBLANES, gw)
        tc = jnp.concatenate([tc] * groups, axis=-1)
        ts = jnp.concatenate([ts] * groups, axis=-1)
        pr = ar * tc + ai * ts
        pi = ai * tc - ar * ts
        a_ref[:, mb] = jnp.concatenate([pr, pi], axis=-1).reshape(nl, SUBLANES, 2 * cw)
    for ab in range(nl // SUBLANES):
        bb = a_ref[pl.ds(ab * SUBLANES, SUBLANES)].reshape(SUBLANES * nm, 2 * cw).astype(BF16)
        res = _dot(k2c_ref[...], bb[:, :cw]) + _dot(k2s_ref[...], bb[:, cw:])
        o_ref[0, :, ab] = res.reshape(nm, SUBLANES, cw)


def _dft_tables(s):
    nm = 64
    nl = s // nm
    assert nl * nm == s and nl % SUBLANES == 0
    eye = np.eye(SUBLANES)

    def dft(n):
        idx = np.arange(n)
        ang = 2.0 * np.pi * ((idx[:, None] * idx[None, :]) % n) / n
        return np.cos(ang), np.sin(ang)

    k1c, k1s = (np.kron(f, eye) for f in dft(nl))
    k2c, k2s = (np.einsum("nm,ij->nijm", f, eye).reshape(nm * SUBLANES, SUBLANES * nm)
                for f in dft(nm))
    a = np.arange(nl)[:, None]
    m = np.arange(nm)[None, :]
    ang = 2.0 * np.pi * ((a * m) % s) / s
    norm = 1.0 / np.sqrt(float(s) * FNET_GW)
    shape = (nl, nm // SUBLANES, SUBLANES, FNET_GW)
    twc = np.broadcast_to((np.cos(ang) * norm).reshape(nl, nm // SUBLANES, SUBLANES, 1), shape)
    tws = np.broadcast_to((np.sin(ang) * norm).reshape(nl, nm // SUBLANES, SUBLANES, 1), shape)
    w = np.arange(FNET_GW)
    angw = 2.0 * np.pi * ((w[:, None] * w[None, :]) % FNET_GW) / FNET_GW
    ww = np.concatenate([np.cos(angw), -np.sin(angw)], axis=1)
    bf = lambda v: jnp.asarray(v, dtype=F32).astype(BF16)
    return (nl, nm, bf(ww), bf(k1c), bf(k1s), bf(k2c), bf(k2s),
            jnp.asarray(twc, dtype=F32), jnp.asarray(tws, dtype=F32))


def _fnet(f, groups):
    b, s, _ = f.shape
    nl, nm, ww, k1c, k1s, k2c, k2s, twc, tws = _dft_tables(s)
    cw = groups * FNET_GW
    consts = (ww, k1c, k1s, k2c, k2s, twc, tws)
    out = pl.pallas_call(
        functools.partial(_fnet_kernel, nl=nl, nm=nm, groups=groups),
        grid=(b, FNET_W // cw),
        in_specs=[pl.BlockSpec((1, s, cw), lambda i, j: (i, 0, j))]
                 + [_const_spec(a.shape) for a in consts],
        out_specs=pl.BlockSpec((1, nm, nl // SUBLANES, SUBLANES, cw), lambda i, j: (i, 0, 0, 0, j)),
        out_shape=jax.ShapeDtypeStruct((b, nm, nl // SUBLANES, SUBLANES, FNET_W), F32),
        scratch_shapes=[pltpu.VMEM((nl, nm // SUBLANES, SUBLANES, 2 * cw), F32),
                        pltpu.VMEM((nl, nm // SUBLANES, SUBLANES, 2 * cw), F32)],
        compiler_params=_params("parallel", "parallel"),
        name="fnet",
    )(f, *consts)
    return out.reshape(b, s, FNET_W)


FFN_SPLIT = 6 * 256


def _swiglu(xs, ng_ref, wg_ref, wu_ref, wd_ref):
    hs = [_rms(x, ng_ref[...]).astype(BF16) for x in xs]
    ys = list(xs)
    for cols in (slice(0, FFN_SPLIT), slice(FFN_SPLIT, FFN_HIDDEN)):
        gu = [(_dot(h, wg_ref[:, cols]), _dot(h, wu_ref[:, cols])) for h in hs]
        acts = [(g * jax.nn.sigmoid(g) * u).astype(BF16) for g, u in gu]
        ys = [y + _dot(a, wd_ref[cols, :]) for y, a in zip(ys, acts)]
    return ys


def _ab_out_ffn_kernel(of_ref, ob_ref, r_ref, fo_ref, x_ref, og_ref, wo_ref, wf_ref,
                       ng_ref, wg_ref, wu_ref, wd_ref, y_ref):
    groups = _row_groups(y_ref)
    xs = []
    for rows in groups:
        o = of_ref[rows, :] + ob_ref[rows, :]
        heads = [_rms(o[:, GLA_DV * h:GLA_DV * (h + 1)], og_ref[...]) for h in range(GLA_HEADS)]
        r = r_ref[rows, :].astype(F32)
        o = jnp.concatenate(heads, axis=-1) * (r * jax.nn.sigmoid(r))
        mixed = _dot(o.astype(BF16), wo_ref[...]) + _dot(fo_ref[rows, :].astype(BF16), wf_ref[...])
        xs.append(x_ref[rows, :] + mixed)
    for rows, y in zip(groups, _swiglu(xs, ng_ref, wg_ref, wu_ref, wd_ref)):
        y_ref[rows, :] = y


def _c_out_ffn_kernel(o_ref, x_ref, w_ref, ng_ref, wg_ref, wu_ref, wd_ref, y_ref):
    groups = _row_groups(y_ref)
    xs = [x_ref[rows, :] + _dot(o_ref[rows, :], w_ref[...]) for rows in groups]
    for rows, y in zip(groups, _swiglu(xs, ng_ref, wg_ref, wu_ref, wd_ref)):
        y_ref[rows, :] = y


def _rows_call(body, name, row_args, const_args, tm):
    t = row_args[0].shape[0]
    row = lambda a: pl.BlockSpec((tm, a.shape[1]), lambda i: (i, 0))
    return pl.pallas_call(
        body,
        grid=(t // tm,),
        in_specs=[row(a) for a in row_args] + [_const_spec(a.shape) for a in const_args],
        out_specs=pl.BlockSpec((tm, D_MODEL), lambda i: (i, 0)),
        out_shape=jax.ShapeDtypeStruct((t, D_MODEL), F32),
        compiler_params=_params("parallel"),
        name=name,
    )(*row_args, *const_args)


SUM_ROWS = 16
_QUARTER = HEAD_DIM // 4
_HEAD_PERM = np.concatenate([np.arange(0, _QUARTER), np.arange(2 * _QUARTER, 3 * _QUARTER),
                             np.arange(_QUARTER, 2 * _QUARTER), np.arange(3 * _QUARTER, 4 * _QUARTER)])


def _qk_norm_rope(y2, ones_bd, gain, cos, sin):
    sq = (y2 * y2).astype(BF16)
    ms = _dot(sq, ones_bd) * (1.0 / HEAD_DIM)
    inv = lax.rsqrt(ms + NORM_EPS)
    halves = []
    for j in range(2):
        yh = y2[:, HEAD_DIM * j:HEAD_DIM * (j + 1)] * gain
        halves.append(yh * cos + pltpu.roll(yh, HEAD_DIM // 2, 1) * sin)
    return jnp.concatenate(halves, axis=-1) * inv


def _c_in_kernel(x_ref, ng_ref, w_ref, bd_ref, qg_ref, kg_ref, cos_ref, sin_ref, q_ref, k_ref, vt_ref):
    h = _rms(x_ref[...], ng_ref[...]).astype(BF16)
    y = _dot(h, w_ref[...])
    ones_bd = bd_ref[...]
    cos, sin = cos_ref[...], sin_ref[...]
    for i in range(0, ATTN_HEADS, 2):
        cols = slice(HEAD_DIM * i, HEAD_DIM * (i + 2))
        q_ref[:, cols] = _qk_norm_rope(y[:, cols], ones_bd, qg_ref[...], cos, sin).astype(BF16)
    k_ref[...] = _qk_norm_rope(y[:, ATTN_Q:ATTN_Q + ATTN_KV], ones_bd, kg_ref[...], cos, sin).astype(BF16)
    for i in range(KV_HEADS):
        vh = y[:, ATTN_Q + ATTN_KV + HEAD_DIM * i:ATTN_Q + ATTN_KV + HEAD_DIM * (i + 1)]
        vt_ref[0, i, :HEAD_DIM, :] = vh.T.astype(BF16)
        vt_ref[0, i, HEAD_DIM:, :] = jnp.ones((SUM_ROWS, vh.shape[0]), BF16)


def _rope_tables(s):
    pos = np.arange(s)
    half = HEAD_DIM // 2
    inv = ROPE_THETA ** (-np.arange(0, half, 2, dtype=np.float64) / half)
    ar = (pos // GRID_W)[:, None] * inv[None, :]
    ac = (pos % GRID_W)[:, None] * inv[None, :]
    ang = np.concatenate([ar, ac, ar, ac], axis=-1)
    sign = np.where(np.arange(HEAD_DIM) < half, -1.0, 1.0)[None, :]
    return jnp.asarray(np.cos(ang), dtype=F32), jnp.asarray(np.sin(ang) * sign, dtype=F32)


def _c_in(x, ng, w, qg, kg, s, tm):
    t = x.shape[0]
    per_seq = s // tm
    row = lambda wd: pl.BlockSpec((tm, wd), lambda i: (i, 0))
    tab = pl.BlockSpec((tm, HEAD_DIM), lambda i: (i % per_seq, 0))
    ones_bd = jnp.asarray(np.kron(np.eye(2), np.ones((HEAD_DIM, HEAD_DIM))), dtype=BF16)
    consts = (ng, w, ones_bd, qg, kg)
    vt_rows = HEAD_DIM + SUM_ROWS
    return pl.pallas_call(
        _c_in_kernel,
        grid=(t // tm,),
        in_specs=[row(D_MODEL)] + [_const_spec(a.shape) for a in consts] + [tab] * 2,
        out_specs=[row(ATTN_Q), row(ATTN_KV),
                   pl.BlockSpec((1, KV_HEADS, vt_rows, tm), lambda i: (i // per_seq, 0, 0, i % per_seq))],
        out_shape=[jax.ShapeDtypeStruct((t, ATTN_Q), BF16), jax.ShapeDtypeStruct((t, ATTN_KV), BF16),
                   jax.ShapeDtypeStruct((t // s, KV_HEADS, vt_rows, s), BF16)],
        compiler_params=_params("parallel"),
        name="c_in",
    )(x, *consts, *_rope_tables(s))


def _attn_kernel(q_ref, k_ref, vt_ref, o_ref, s_ref, m_ref, *, tq):
    group = ATTN_HEADS // KV_HEADS
    nblk = q_ref.shape[1] // tq

    def rows_of(r):
        return pl.ds(pl.multiple_of(r * tq, tq), tq)

    def scores(r, g):
        q = q_ref[0, rows_of(r), HEAD_DIM * g:HEAD_DIM * (g + 1)]
        s_ref[g] = lax.dot_general(k_ref[0], q, _NT, preferred_element_type=F32)

    def col_max(g):
        m_ref[g] = jnp.max(s_ref[g], axis=0, keepdims=True)

    def finish(r, g):
        ov = _dot(vt_ref[0, 0], jnp.exp2(s_ref[g] - m_ref[g]).astype(BF16))
        out = ov[:HEAD_DIM] / ov[HEAD_DIM:HEAD_DIM + 1]
        o_ref[0, rows_of(r), HEAD_DIM * g:HEAD_DIM * (g + 1)] = out.T.astype(o_ref.dtype)

    scores(0, 0)
    scores(0, 1)
    col_max(0)

    def body(r, carry):
        nxt = jnp.minimum(r + 1, nblk - 1)
        for g in range(group):
            scores(r if g + 2 < group else nxt, (g + 2) % group)
            col_max((g + 1) % group)
            finish(r, g)
        return carry

    lax.fori_loop(0, nblk, body, 0, unroll=4)


def _attn(q, k, vt, tq):
    b, s, _ = q.shape
    gw = ATTN_Q // KV_HEADS
    return pl.pallas_call(
        functools.partial(_attn_kernel, tq=tq),
        grid=(b, KV_HEADS),
        in_specs=[pl.BlockSpec((1, s, gw), lambda i, j: (i, 0, j)),
                  pl.BlockSpec((1, s, HEAD_DIM), lambda i, j: (i, 0, j)),
                  pl.BlockSpec((1, 1, HEAD_DIM + SUM_ROWS, s), lambda i, j: (i, j, 0, 0))],
        out_specs=pl.BlockSpec((1, s, gw), lambda i, j: (i, 0, j)),
        out_shape=jax.ShapeDtypeStruct((b, s, ATTN_Q), BF16),
        scratch_shapes=[pltpu.VMEM((ATTN_HEADS // KV_HEADS, s, tq), F32),
                        pltpu.VMEM((ATTN_HEADS // KV_HEADS, 1, tq), F32)],
        compiler_params=_params("parallel", "parallel"),
        name="attn",
    )(q, k, vt)


def _prep_weights(ab_norm, ab_w_in, gla_up_f, gla_bias_f, gla_up_b, gla_bias_b, gla_out_norm,
                  ab_w_out, c_norm, c_w_in, c_q_norm, c_k_norm, c_w_out, ffn_norm,
                  ffn_w_gate, ffn_w_up, ffn_w_down):
    n_main = 2 * GLA_QK + 2 * GLA_V
    w_in = ab_w_in[0]
    zeros = jnp.zeros((GLA_RANK, GLA_QK), F32)
    up = jnp.concatenate([jnp.concatenate([gla_up_f[0], zeros], axis=1),
                          jnp.concatenate([zeros, gla_up_b[0]], axis=1)], axis=0)
    qk_w = ATTN_Q + ATTN_KV
    c_w = c_w_in[0].astype(BF16)
    c_qk = c_w[:, :qk_w].reshape(D_MODEL, qk_w // HEAD_DIM, 2, 2, _QUARTER).swapaxes(2, 3)
    c_w = jnp.concatenate([c_qk.reshape(D_MODEL, qk_w), c_w[:, qk_w:]], axis=1)
    return dict(
        ab_ng=ab_norm[0][None, :],
        ab_wm=w_in[:, :n_main].astype(BF16),
        ab_wl=w_in[:, n_main:n_main + 2 * GLA_RANK].astype(BF16),
        ab_wf=w_in[:, n_main + 2 * GLA_RANK:].astype(BF16),
        ab_up=up,
        ab_bias=jnp.concatenate([gla_bias_f[0], gla_bias_b[0]])[None, :],
        ab_og=gla_out_norm[0][None, :],
        ab_wo=ab_w_out[0][:GLA_V].astype(BF16),
        ab_wfo=ab_w_out[0][GLA_V:].astype(BF16),
        c_ng=c_norm[0][None, :],
        c_w=c_w,
        c_qg=(c_q_norm[0][_HEAD_PERM] * ((HEAD_DIM ** -0.5) * float(np.log2(np.e))))[None, :],
        c_kg=c_k_norm[0][_HEAD_PERM][None, :],
        c_wo=c_w_out[0].astype(BF16),
        ffn_ng=[ffn_norm[i][None, :] for i in range(2)],
        ffn_wg=[ffn_w_gate[i].astype(BF16) for i in range(2)],
        ffn_wu=[ffn_w_up[i].astype(BF16) for i in range(2)],
        ffn_wd=[ffn_w_down[i].astype(BF16) for i in range(2)],
    )


def _trunk(x, w, tm=1024, tm_ffn=512, gla_blk=256, tq=256, fnet_groups=2):
    b, s, d = x.shape
    t = b * s
    xf = x.reshape(t, d)
    q, k, v, r, gf, gb, f = _ab_in(xf, w["ab_ng"], w["ab_wm"], w["ab_wl"], w["ab_wf"],
                                   w["ab_up"], w["ab_bias"], tm)
    seq = lambda a: a.reshape(b, s, a.shape[-1])
    o_f, o_b = _gla(seq(q), seq(k), seq(v), seq(gf), seq(gb), gla_blk)
    fo = _fnet(seq(f), fnet_groups)
    ffn = lambda i: (w["ffn_ng"][i], w["ffn_wg"][i], w["ffn_wu"][i], w["ffn_wd"][i])
    xf = _rows_call(_ab_out_ffn_kernel, "ab_out_ffn",
                    (o_f.reshape(t, GLA_V), o_b.reshape(t, GLA_V), r, fo.reshape(t, FNET_W), xf),
                    (w["ab_og"], w["ab_wo"], w["ab_wfo"]) + ffn(0), tm_ffn)
    q, k, vt = _c_in(xf, w["c_ng"], w["c_w"], w["c_qg"], w["c_kg"], s, tm)
    o = _attn(seq(q), seq(k), vt, tq)
    xf = _rows_call(_c_out_ffn_kernel, "c_out_ffn", (o.reshape(t, ATTN_Q), xf), (w["c_wo"],) + ffn(1), tm_ffn)
    return xf.reshape(b, s, d)


def kernel(x_prompt, x_sample, ab_norm, ab_w_in, gla_up_f, gla_bias_f, gla_up_b, gla_bias_b,
           gla_out_norm, ab_w_out, c_norm, c_w_in, c_q_norm, c_k_norm, c_w_out, ffn_norm,
           ffn_w_gate, ffn_w_up, ffn_w_down):
    w = _prep_weights(ab_norm, ab_w_in, gla_up_f, gla_bias_f, gla_up_b, gla_bias_b, gla_out_norm,
                      ab_w_out, c_norm, c_w_in, c_q_norm, c_k_norm, c_w_out, ffn_norm,
                      ffn_w_gate, ffn_w_up, ffn_w_down)
    return _trunk(x_prompt, w), _trunk(x_sample, w)
```

```python
import functools

import numpy as np
import jax
import jax.numpy as jnp
from jax import lax
from jax.experimental import pallas as pl
from jax.experimental.pallas import tpu as pltpu

D_MODEL = 1024
GRID_W = 64
NORM_EPS = 1e-6
GLA_HEADS = 4
GLA_DK = 64
GLA_DV = 128
GLA_RANK = 16
GLA_GATE_NORM = 16.0
GLA_CHUNK = 64
FNET_GROUPS = 4
FNET_GW = 128
ATTN_HEADS = 8
KV_HEADS = 2
HEAD_DIM = 128
ROPE_THETA = 10000.0
FFN_HIDDEN = -(-8 * D_MODEL // (3 * 256)) * 256

GLA_QK = GLA_HEADS * GLA_DK
GLA_V = GLA_HEADS * GLA_DV
FNET_W = FNET_GROUPS * FNET_GW
ATTN_Q = ATTN_HEADS * HEAD_DIM
ATTN_KV = KV_HEADS * HEAD_DIM

SUBLANES = 8
VMEM_LIMIT = 56 * 1024 * 1024

BF16 = jnp.bfloat16
F32 = jnp.float32

_NT = (((1,), (1,)), ((), ()))
_TN = (((0,), (0,)), ((), ()))


def _params(*sem):
    return pltpu.CompilerParams(dimension_semantics=sem, vmem_limit_bytes=VMEM_LIMIT)


def _const_spec(shape):
    nd = len(shape)
    return pl.BlockSpec(shape, lambda *_: (0,) * nd, pipeline_mode=pl.Buffered(1))


def _dot(a, b):
    return jnp.dot(a, b, preferred_element_type=F32)


def _split_bf16(a):
    hi = a.astype(BF16)
    lo = (a - hi.astype(F32)).astype(BF16)
    return hi, lo


def _row_groups(ref, groups=2):
    n = ref.shape[0] // groups
    return [pl.ds(i * n, n) for i in range(groups)]


def _rms(x, gain):
    return x * lax.rsqrt(jnp.mean(x * x, axis=-1, keepdims=True) + NORM_EPS) * gain


def _ab_in_kernel(x_ref, ng_ref, wm_ref, wl_ref, wf_ref, up_ref, bias_ref,
                  q_ref, k_ref, v_ref, r_ref, gf_ref, gb_ref, f_ref):
    h = _rms(x_ref[...], ng_ref[...]).astype(BF16)
    low3 = _dot(h, wl_ref[...])
    y = _dot(h, wm_ref[...])
    q_ref[...] = (y[:, :GLA_QK] * (GLA_DK ** -0.5)).astype(BF16)
    k_ref[...] = y[:, GLA_QK:2 * GLA_QK].astype(BF16)
    v_ref[...] = y[:, 2 * GLA_QK:2 * GLA_QK + GLA_V].astype(BF16)
    r_ref[...] = y[:, 2 * GLA_QK + GLA_V:].astype(BF16)
    low_hi, low_lo = _split_bf16(low3)
    lane = lax.broadcasted_iota(jnp.int32, (1, 6 * GLA_RANK), 1)
    mid = (lane >= 2 * GLA_RANK) & (lane < 4 * GLA_RANK)
    up_hi, up_lo = _split_bf16(up_ref[...])
    z = _dot(jnp.where(mid, low_lo, low_hi), jnp.concatenate([up_hi, up_hi, up_lo], axis=0)) + bias_ref[...]
    g = (jnp.minimum(z, 0.0) - jnp.log(1.0 + jnp.exp(-jnp.abs(z)))) * (1.0 / GLA_GATE_NORM)
    gf_ref[...] = g[:, :GLA_QK]
    gb_ref[...] = g[:, GLA_QK:]
    f_ref[...] = _dot(h, wf_ref[...]).astype(BF16)


def _ab_in(x, ng, wm, wl, wf, up, bias, tm):
    t = x.shape[0]
    row = lambda w: pl.BlockSpec((tm, w), lambda i: (i, 0))
    outs = [(GLA_QK, BF16), (GLA_QK, BF16), (GLA_V, BF16), (GLA_V, BF16),
            (GLA_QK, F32), (GLA_QK, F32), (FNET_W, BF16)]
    return pl.pallas_call(
        _ab_in_kernel,
        grid=(t // tm,),
        in_specs=[row(D_MODEL)] + [_const_spec(a.shape) for a in (ng, wm, wl, wf, up, bias)],
        out_specs=[row(w) for w, _ in outs],
        out_shape=[jax.ShapeDtypeStruct((t, w), d) for w, d in outs],
        compiler_params=_params("parallel"),
        name="ab_in",
    )(x, ng, wm, wl, wf, up, bias)


def _gla_local(q_ref, k_ref, v_ref, g_ref, *, chunks, reverse):
    c = GLA_CHUNK
    n = chunks * c
    pair_w = 2 * GLA_DK
    ri = lax.broadcasted_iota(jnp.int32, (n, n), 0)
    ci = lax.broadcasted_iota(jnp.int32, (n, n), 1)
    causal = (ri <= ci) if reverse else (ri >= ci)
    tri = jnp.where(causal & ((ri // c) == (ci // c)), 1.0, 0.0).astype(BF16)
    g_hi, g_lo = _split_bf16(g_ref[0])
    bc = _dot(tri, g_hi) + _dot(tri, g_lo)
    q_in = (q_ref[0].astype(F32) * jnp.exp(bc)).astype(BF16)
    k_in = k_ref[0].astype(F32) * jnp.exp(-bc)
    v = v_ref[0]
    last = [j * c if reverse else (j + 1) * c - 1 for j in range(chunks)]
    tot = jnp.concatenate([bc[t:t + 1, :] for t in last] + [jnp.zeros((pair_w - chunks, GLA_QK), F32)],
                          axis=0)
    e_row = jnp.exp(tot)
    e_col = jnp.exp(tot.T)
    r2 = lax.broadcasted_iota(jnp.int32, (c, pair_w), 0)
    c2 = lax.broadcasted_iota(jnp.int32, (c, pair_w), 1) % c
    keep = (r2 <= c2) if reverse else (r2 >= c2)
    first = lax.broadcasted_iota(jnp.int32, (1, pair_w), 1) < GLA_DK
    rb = lax.broadcasted_iota(jnp.int32, (pair_w, 2 * GLA_DV), 0) < GLA_DK
    cb = lax.broadcasted_iota(jnp.int32, (pair_w, 2 * GLA_DV), 1) < GLA_DV
    diag = rb == cb
    local = {}
    for j in range(chunks):
        rows = slice(j * c, (j + 1) * c)
        k_c = k_in[rows]
        k_dec = (k_c * e_row[j:j + 1, :]).astype(BF16)
        for p in range(GLA_HEADS // 2):
            pair = slice(pair_w * p, pair_w * (p + 1))
            kp = k_c[:, pair]
            k_bd = jnp.concatenate([jnp.where(first, kp, 0.0), jnp.where(first, 0.0, kp)], axis=0).astype(BF16)
            qp = q_in[rows, pair]
            att = lax.dot_general(qp, k_bd, _NT, preferred_element_type=F32)
            att = jnp.where(keep, att, 0.0).astype(BF16)
            vp = v[rows, 2 * GLA_DV * p:2 * GLA_DV * (p + 1)]
            v_bd = jnp.where(diag, jnp.concatenate([vp, vp], axis=0), jnp.zeros((), BF16))
            upd = lax.dot_general(k_dec[:, pair], vp, _TN, preferred_element_type=F32)
            local[j, p] = (jnp.concatenate([att, qp], axis=1), v_bd, jnp.where(diag, upd, 0.0),
                           e_col[pair, j:j + 1])
    return local


def _gla_serial(local, o_ref, state_ref, *, chunks, reverse):
    c = GLA_CHUNK
    states = [state_ref[p] for p in range(GLA_HEADS // 2)]
    for j in (reversed(range(chunks)) if reverse else range(chunks)):
        outs = []
        for p in range(GLA_HEADS // 2):
            lhs, v_bd, upd, decay = local[j, p]
            outs.append(_dot(lhs, jnp.concatenate([v_bd, states[p].astype(BF16)], axis=0)))
            states[p] = states[p] * decay + upd
        o_ref[0, j * c:(j + 1) * c, :] = jnp.concatenate(outs, axis=-1)
    for p in range(GLA_HEADS // 2):
        state_ref[p] = states[p]


def _gla_kernel(qf_ref, kf_ref, vf_ref, gf_ref, qb_ref, kb_ref, vb_ref, gb_ref,
                of_ref, ob_ref, sf_ref, sb_ref, *, chunks):
    @pl.when(pl.program_id(1) == 0)
    def _():
        sf_ref[...] = jnp.zeros_like(sf_ref)
        sb_ref[...] = jnp.zeros_like(sb_ref)

    fwd = _gla_local(qf_ref, kf_ref, vf_ref, gf_ref, chunks=chunks, reverse=False)
    bwd = _gla_local(qb_ref, kb_ref, vb_ref, gb_ref, chunks=chunks, reverse=True)
    _gla_serial(fwd, of_ref, sf_ref, chunks=chunks, reverse=False)
    _gla_serial(bwd, ob_ref, sb_ref, chunks=chunks, reverse=True)


def _gla(q, k, v, gf, gb, blk):
    b, s, _ = q.shape
    nb = s // blk
    fwd = lambda w: pl.BlockSpec((1, blk, w), lambda i, n: (i, n, 0))
    bwd = lambda w: pl.BlockSpec((1, blk, w), lambda i, n: (i, nb - 1 - n, 0))
    state = pltpu.VMEM((GLA_HEADS // 2, 2 * GLA_DK, 2 * GLA_DV), F32)
    return pl.pallas_call(
        functools.partial(_gla_kernel, chunks=blk // GLA_CHUNK),
        grid=(b, nb),
        in_specs=[fwd(GLA_QK), fwd(GLA_QK), fwd(GLA_V), fwd(GLA_QK),
                  bwd(GLA_QK), bwd(GLA_QK), bwd(GLA_V), bwd(GLA_QK)],
        out_specs=[fwd(GLA_V), bwd(GLA_V)],
        out_shape=[jax.ShapeDtypeStruct((b, s, GLA_V), F32)] * 2,
        scratch_shapes=[state, state],
        compiler_params=_params("parallel", "arbitrary"),
        name="gla",
    )(q, k, v, gf, q, k, v, gb)


def _fnet_kernel(x_ref, ww_ref, k1c_ref, k1s_ref, k2c_ref, k2s_ref, twc_ref, tws_ref,
                 o_ref, z_ref, a_ref, *, nl, nm, groups):
    gw = FNET_GW
    cw = groups * gw
    rows_per = SUBLANES * nm
    for i in range(nl // SUBLANES):
        xs = x_ref[0, pl.ds(i * rows_per, rows_per), :]
        zr, zi = [], []
        for g in range(groups):
            zz = _dot(xs[:, g * gw:(g + 1) * gw], ww_ref[...])
            zr.append(zz[:, :gw])
            zi.append(zz[:, gw:])
        zcat = jnp.concatenate(zr + zi, axis=-1)
        z_ref[pl.ds(i * SUBLANES, SUBLANES)] = zcat.reshape(SUBLANES, nm // SUBLANES, SUBLANES, 2 * cw)
    for mb in range(nm // SUBLANES):
        z = z_ref[:, mb].reshape(nl * SUBLANES, 2 * cw)
        zr, zi = z[:, :cw], z[:, cw:]
        za = z.astype(BF16)
        zb = jnp.concatenate([zi, -zr], axis=-1).astype(BF16)
        a = _dot(k1c_ref[...], za) + _dot(k1s_ref[...], zb)
        ar, ai = a[:, :cw], a[:, cw:]
        tc = twc_ref[:, mb].reshape(nl * SUBLANES, gw)
        ts = tws_ref[:, mb].reshape(nl * SUBLANES, gw)
        tc = jnp.concatenate([tc] * groups, axis=-1)
        ts = jnp.concatenate([ts] * groups, axis=-1)
        pr = ar * tc + ai * ts
        pi = ai * tc - ar * ts
        a_ref[:, mb] = jnp.concatenate([pr, pi], axis=-1).reshape(nl, SUBLANES, 2 * cw)
    for ab in range(nl // SUBLANES):
        bb = a_ref[pl.ds(ab * SUBLANES, SUBLANES)].reshape(SUBLANES * nm, 2 * cw).astype(BF16)
        res = _dot(k2c_ref[...], bb[:, :cw]) + _dot(k2s_ref[...], bb[:, cw:])
        o_ref[0, :, ab] = res.reshape(nm, SUBLANES, cw)


def _dft_tables(s):
    nm = 64
    nl = s // nm
    assert nl * nm == s and nl % SUBLANES == 0
    eye = np.eye(SUBLANES)

    def dft(n):
        idx = np.arange(n)
        ang = 2.0 * np.pi * ((idx[:, None] * idx[None, :]) % n) / n
        return np.cos(ang), np.sin(ang)

    k1c, k1s = (np.kron(f, eye) for f in dft(nl))
    k2c, k2s = (np.einsum("nm,ij->nijm", f, eye).reshape(nm * SUBLANES, SUBLANES * nm)
                for f in dft(nm))
    a = np.arange(nl)[:, None]
    m = np.arange(nm)[None, :]
    ang = 2.0 * np.pi * ((a * m) % s) / s
    norm = 1.0 / np.sqrt(float(s) * FNET_GW)
    shape = (nl, nm // SUBLANES, SUBLANES, FNET_GW)
    twc = np.broadcast_to((np.cos(ang) * norm).reshape(nl, nm // SUBLANES, SUBLANES, 1), shape)
    tws = np.broadcast_to((np.sin(ang) * norm).reshape(nl, nm // SUBLANES, SUBLANES, 1), shape)
    w = np.arange(FNET_GW)
    angw = 2.0 * np.pi * ((w[:, None] * w[None, :]) % FNET_GW) / FNET_GW
    ww = np.concatenate([np.cos(angw), -np.sin(angw)], axis=1)
    bf = lambda v: jnp.asarray(v, dtype=F32).astype(BF16)
    return (nl, nm, bf(ww), bf(k1c), bf(k1s), bf(k2c), bf(k2s),
            jnp.asarray(twc, dtype=F32), jnp.asarray(tws, dtype=F32))


def _fnet(f, groups):
    b, s, _ = f.shape
    nl, nm, ww, k1c, k1s, k2c, k2s, twc, tws = _dft_tables(s)
    cw = groups * FNET_GW
    consts = (ww, k1c, k1s, k2c, k2s, twc, tws)
    out = pl.pallas_call(
        functools.partial(_fnet_kernel, nl=nl, nm=nm, groups=groups),
        grid=(b, FNET_W // cw),
        in_specs=[pl.BlockSpec((1, s, cw), lambda i, j: (i, 0, j))]
                 + [_const_spec(a.shape) for a in consts],
        out_specs=pl.BlockSpec((1, nm, nl // SUBLANES, SUBLANES, cw), lambda i, j: (i, 0, 0, 0, j)),
        out_shape=jax.ShapeDtypeStruct((b, nm, nl // SUBLANES, SUBLANES, FNET_W), F32),
        scratch_shapes=[pltpu.VMEM((nl, nm // SUBLANES, SUBLANES, 2 * cw), F32),
                        pltpu.VMEM((nl, nm // SUBLANES, SUBLANES, 2 * cw), F32)],
        compiler_params=_params("parallel", "parallel"),
        name="fnet",
    )(f, *consts)
    return out.reshape(b, s, FNET_W)


FFN_SPLIT = 6 * 256


def _swiglu(xs, ng_ref, wg_ref, wu_ref, wd_ref):
    hs = [_rms(x, ng_ref[...]).astype(BF16) for x in xs]
    ys = list(xs)
    for cols in (slice(0, FFN_SPLIT), slice(FFN_SPLIT, FFN_HIDDEN)):
        gu = [(_dot(h, wg_ref[:, cols]), _dot(h, wu_ref[:, cols])) for h in hs]
        acts = [(g * jax.nn.sigmoid(g) * u).astype(BF16) for g, u in gu]
        ys = [y + _dot(a, wd_ref[cols, :]) for y, a in zip(ys, acts)]
    return ys


def _ab_out_ffn_kernel(of_ref, ob_ref, r_ref, fo_ref, x_ref, og_ref, wo_ref, wf_ref,
                       ng_ref, wg_ref, wu_ref, wd_ref, y_ref):
    groups = _row_groups(y_ref)
    xs = []
    for rows in groups:
        o = of_ref[rows, :] + ob_ref[rows, :]
        heads = [_rms(o[:, GLA_DV * h:GLA_DV * (h + 1)], og_ref[...]) for h in range(GLA_HEADS)]
        r = r_ref[rows, :].astype(F32)
        o = jnp.concatenate(heads, axis=-1) * (r * jax.nn.sigmoid(r))
        mixed = _dot(o.astype(BF16), wo_ref[...]) + _dot(fo_ref[rows, :].astype(BF16), wf_ref[...])
        xs.append(x_ref[rows, :] + mixed)
    for rows, y in zip(groups, _swiglu(xs, ng_ref, wg_ref, wu_ref, wd_ref)):
        y_ref[rows, :] = y


def _c_out_ffn_kernel(o_ref, x_ref, w_ref, ng_ref, wg_ref, wu_ref, wd_ref, y_ref):
    groups = _row_groups(y_ref)
    xs = [x_ref[rows, :] + _dot(o_ref[rows, :], w_ref[...]) for rows in groups]
    for rows, y in zip(groups, _swiglu(xs, ng_ref, wg_ref, wu_ref, wd_ref)):
        y_ref[rows, :] = y


def _rows_call(body, name, row_args, const_args, tm):
    t = row_args[0].shape[0]
    row = lambda a: pl.BlockSpec((tm, a.shape[1]), lambda i: (i, 0))
    return pl.pallas_call(
        body,
        grid=(t // tm,),
        in_specs=[row(a) for a in row_args] + [_const_spec(a.shape) for a in const_args],
        out_specs=pl.BlockSpec((tm, D_MODEL), lambda i: (i, 0)),
        out_shape=jax.ShapeDtypeStruct((t, D_MODEL), F32),
        compiler_params=_params("parallel"),
        name=name,
    )(*row_args, *const_args)


SUM_ROWS = 16
_QUARTER = HEAD_DIM // 4
_HEAD_PERM = np.concatenate([np.arange(0, _QUARTER), np.arange(2 * _QUARTER, 3 * _QUARTER),
                             np.arange(_QUARTER, 2 * _QUARTER), np.arange(3 * _QUARTER, 4 * _QUARTER)])


def _qk_norm_rope(y2, ones_bd, gain, cos, sin):
    sq = (y2 * y2).astype(BF16)
    ms = _dot(sq, ones_bd) * (1.0 / HEAD_DIM)
    inv = lax.rsqrt(ms + NORM_EPS)
    halves = []
    for j in range(2):
        yh = y2[:, HEAD_DIM * j:HEAD_DIM * (j + 1)] * gain
        halves.append(yh * cos + pltpu.roll(yh, HEAD_DIM // 2, 1) * sin)
    return jnp.concatenate(halves, axis=-1) * inv


def _c_in_kernel(x_ref, ng_ref, w_ref, bd_ref, qg_ref, kg_ref, cos_ref, sin_ref, q_ref, k_ref, vt_ref):
    h = _rms(x_ref[...], ng_ref[...]).astype(BF16)
    y = _dot(h, w_ref[...])
    ones_bd = bd_ref[...]
    cos, sin = cos_ref[...], sin_ref[...]
    for i in range(0, ATTN_HEADS, 2):
        cols = slice(HEAD_DIM * i, HEAD_DIM * (i + 2))
        q_ref[:, cols] = _qk_norm_rope(y[:, cols], ones_bd, qg_ref[...], cos, sin).astype(BF16)
    k_ref[...] = _qk_norm_rope(y[:, ATTN_Q:ATTN_Q + ATTN_KV], ones_bd, kg_ref[...], cos, sin).astype(BF16)
    for i in range(KV_HEADS):
        vh = y[:, ATTN_Q + ATTN_KV + HEAD_DIM * i:ATTN_Q + ATTN_KV + HEAD_DIM * (i + 1)]
        vt_ref[0, i, :HEAD_DIM, :] = vh.T.astype(BF16)
        vt_ref[0, i, HEAD_DIM:, :] = jnp.ones((SUM_ROWS, vh.shape[0]), BF16)


def _rope_tables(s):
    pos = np.arange(s)
    half = HEAD_DIM // 2
    inv = ROPE_THETA ** (-np.arange(0, half, 2, dtype=np.float64) / half)
    ar = (pos // GRID_W)[:, None] * inv[None, :]
    ac = (pos % GRID_W)[:, None] * inv[None, :]
    ang = np.concatenate([ar, ac, ar, ac], axis=-1)
    sign = np.where(np.arange(HEAD_DIM) < half, -1.0, 1.0)[None, :]
    return jnp.asarray(np.cos(ang), dtype=F32), jnp.asarray(np.sin(ang) * sign, dtype=F32)


def _c_in(x, ng, w, qg, kg, s, tm):
    t = x.shape[0]
    per_seq = s // tm
    row = lambda wd: pl.BlockSpec((tm, wd), lambda i: (i, 0))
    tab = pl.BlockSpec((tm, HEAD_DIM), lambda i: (i % per_seq, 0))
    ones_bd = jnp.asarray(np.kron(np.eye(2), np.ones((HEAD_DIM, HEAD_DIM))), dtype=BF16)
    consts = (ng, w, ones_bd, qg, kg)
    vt_rows = HEAD_DIM + SUM_ROWS
    return pl.pallas_call(
        _c_in_kernel,
        grid=(t // tm,),
        in_specs=[row(D_MODEL)] + [_const_spec(a.shape) for a in consts] + [tab] * 2,
        out_specs=[row(ATTN_Q), row(ATTN_KV),
                   pl.BlockSpec((1, KV_HEADS, vt_rows, tm), lambda i: (i // per_seq, 0, 0, i % per_seq))],
        out_shape=[jax.ShapeDtypeStruct((t, ATTN_Q), BF16), jax.ShapeDtypeStruct((t, ATTN_KV), BF16),
                   jax.ShapeDtypeStruct((t // s, KV_HEADS, vt_rows, s), BF16)],
        compiler_params=_params("parallel"),
        name="c_in",
    )(x, *consts, *_rope_tables(s))


def _attn_kernel(q_ref, k_ref, vt_ref, o_ref, s_ref, m_ref, *, tq):
    group = ATTN_HEADS // KV_HEADS
    nblk = q_ref.shape[1] // tq

    def rows_of(r):
        return pl.ds(pl.multiple_of(r * tq, tq), tq)

    def scores(r, g):
        q = q_ref[0, rows_of(r), HEAD_DIM * g:HEAD_DIM * (g + 1)]
        s_ref[g] = lax.dot_general(k_ref[0], q, _NT, preferred_element_type=F32)

    def col_max(g):
        m_ref[g] = jnp.max(s_ref[g], axis=0, keepdims=True)

    def finish(r, g):
        ov = _dot(vt_ref[0, 0], jnp.exp2(s_ref[g] - m_ref[g]).astype(BF16))
        out = ov[:HEAD_DIM] / ov[HEAD_DIM:HEAD_DIM + 1]
        o_ref[0, rows_of(r), HEAD_DIM * g:HEAD_DIM * (g + 1)] = out.T.astype(o_ref.dtype)

    scores(0, 0)
    scores(0, 1)
    col_max(0)

    def body(r, carry):
        nxt = jnp.minimum(r + 1, nblk - 1)
        for g in range(group):
            scores(r if g + 2 < group else nxt, (g + 2) % group)
            col_max((g + 1) % group)
            finish(r, g)
        return carry

    lax.fori_loop(0, nblk, body, 0, unroll=4)


def _attn(q, k, vt, tq):
    b, s, _ = q.shape
    gw = ATTN_Q // KV_HEADS
    return pl.pallas_call(
        functools.partial(_attn_kernel, tq=tq),
        grid=(b, KV_HEADS),
        in_specs=[pl.BlockSpec((1, s, gw), lambda i, j: (i, 0, j)),
                  pl.BlockSpec((1, s, HEAD_DIM), lambda i, j: (i, 0, j)),
                  pl.BlockSpec((1, 1, HEAD_DIM + SUM_ROWS, s), lambda i, j: (i, j, 0, 0))],
        out_specs=pl.BlockSpec((1, s, gw), lambda i, j: (i, 0, j)),
        out_shape=jax.ShapeDtypeStruct((b, s, ATTN_Q), BF16),
        scratch_shapes=[pltpu.VMEM((ATTN_HEADS // KV_HEADS, s, tq), F32),
                        pltpu.VMEM((ATTN_HEADS // KV_HEADS, 1, tq), F32)],
        compiler_params=_params("parallel", "parallel"),
        name="attn",
    )(q, k, vt)


def _prep_weights(ab_norm, ab_w_in, gla_up_f, gla_bias_f, gla_up_b, gla_bias_b, gla_out_norm,
                  ab_w_out, c_norm, c_w_in, c_q_norm, c_k_norm, c_w_out, ffn_norm,
                  ffn_w_gate, ffn_w_up, ffn_w_down):
    n_main = 2 * GLA_QK + 2 * GLA_V
    w_in = ab_w_in[0]
    zeros = jnp.zeros((GLA_RANK, GLA_QK), F32)
    up = jnp.concatenate([jnp.concatenate([gla_up_f[0], zeros], axis=1),
                          jnp.concatenate([zeros, gla_up_b[0]], axis=1)], axis=0)
    qk_w = ATTN_Q + ATTN_KV
    c_w = c_w_in[0].astype(BF16)
    c_qk = c_w[:, :qk_w].reshape(D_MODEL, qk_w // HEAD_DIM, 2, 2, _QUARTER).swapaxes(2, 3)
    c_w = jnp.concatenate([c_qk.reshape(D_MODEL, qk_w), c_w[:, qk_w:]], axis=1)
    return dict(
        ab_ng=ab_norm[0][None, :],
        ab_wm=w_in[:, :n_main].astype(BF16),
        ab_wl=jnp.tile(w_in[:, n_main:n_main + 2 * GLA_RANK].astype(BF16), (1, 3)),
        ab_wf=w_in[:, n_main + 2 * GLA_RANK:].astype(BF16),
        ab_up=up,
        ab_bias=jnp.concatenate([gla_bias_f[0], gla_bias_b[0]])[None, :],
        ab_og=gla_out_norm[0][None, :],
        ab_wo=ab_w_out[0][:GLA_V].astype(BF16),
        ab_wfo=ab_w_out[0][GLA_V:].astype(BF16),
        c_ng=c_norm[0][None, :],
        c_w=c_w,
        c_qg=(c_q_norm[0][_HEAD_PERM] * ((HEAD_DIM ** -0.5) * float(np.log2(np.e))))[None, :],
        c_kg=c_k_norm[0][_HEAD_PERM][None, :],
        c_wo=c_w_out[0].astype(BF16),
        ffn_ng=[ffn_norm[i][None, :] for i in range(2)],
        ffn_wg=[ffn_w_gate[i].astype(BF16) for i in range(2)],
        ffn_wu=[ffn_w_up[i].astype(BF16) for i in range(2)],
        ffn_wd=[ffn_w_down[i].astype(BF16) for i in range(2)],
    )


def _trunk(x, w, tm=1024, tm_ffn=512, gla_blk=256, tq=256, fnet_groups=2):
    b, s, d = x.shape
    t = b * s
    xf = x.reshape(t, d)
    q, k, v, r, gf, gb, f = _ab_in(xf, w["ab_ng"], w["ab_wm"], w["ab_wl"], w["ab_wf"],
                                   w["ab_up"], w["ab_bias"], tm)
    seq = lambda a: a.reshape(b, s, a.shape[-1])
    o_f, o_b = _gla(seq(q), seq(k), seq(v), seq(gf), seq(gb), gla_blk)
    fo = _fnet(seq(f), fnet_groups)
    ffn = lambda i: (w["ffn_ng"][i], w["ffn_wg"][i], w["ffn_wu"][i], w["ffn_wd"][i])
    xf = _rows_call(_ab_out_ffn_kernel, "ab_out_ffn",
                    (o_f.reshape(t, GLA_V), o_b.reshape(t, GLA_V), r, fo.reshape(t, FNET_W), xf),
                    (w["ab_og"], w["ab_wo"], w["ab_wfo"]) + ffn(0), tm_ffn)
    q, k, vt = _c_in(xf, w["c_ng"], w["c_w"], w["c_qg"], w["c_kg"], s, tm)
    o = _attn(seq(q), seq(k), vt, tq)
    xf = _rows_call(_c_out_ffn_kernel, "c_out_ffn", (o.reshape(t, ATTN_Q), xf), (w["c_wo"],) + ffn(1), tm_ffn)
    return xf.reshape(b, s, d)


def kernel(x_prompt, x_sample, ab_norm, ab_w_in, gla_up_f, gla_bias_f, gla_up_b, gla_bias_b,
           gla_out_norm, ab_w_out, c_norm, c_w_in, c_q_norm, c_k_norm, c_w_out, ffn_norm,
           ffn_w_gate, ffn_w_up, ffn_w_down):
    w = _prep_weights(ab_norm, ab_w_in, gla_up_f, gla_bias_f, gla_up_b, gla_bias_b, gla_out_norm,
                      ab_w_out, c_norm, c_w_in, c_q_norm, c_k_norm, c_w_out, ffn_norm,
                      ffn_w_gate, ffn_w_up, ffn_w_down)
    return _trunk(x_prompt, w), _trunk(x_sample, w)
```

```python
import functools

import numpy as np
import jax
import jax.numpy as jnp
from jax import lax
from jax.experimental import pallas as pl
from jax.experimental.pallas import tpu as pltpu

D_MODEL = 1024
GRID_W = 64
NORM_EPS = 1e-6
GLA_HEADS = 4
GLA_DK = 64
GLA_DV = 128
GLA_RANK = 16
GLA_GATE_NORM = 16.0
GLA_CHUNK = 64
FNET_GROUPS = 4
FNET_GW = 128
ATTN_HEADS = 8
KV_HEADS = 2
HEAD_DIM = 128
ROPE_THETA = 10000.0
FFN_HIDDEN = -(-8 * D_MODEL // (3 * 256)) * 256

GLA_QK = GLA_HEADS * GLA_DK
GLA_V = GLA_HEADS * GLA_DV
FNET_W = FNET_GROUPS * FNET_GW
ATTN_Q = ATTN_HEADS * HEAD_DIM
ATTN_KV = KV_HEADS * HEAD_DIM

SUBLANES = 8
VMEM_LIMIT = 56 * 1024 * 1024

BF16 = jnp.bfloat16
F32 = jnp.float32

_NT = (((1,), (1,)), ((), ()))
_TN = (((0,), (0,)), ((), ()))


def _params(*sem):
    return pltpu.CompilerParams(dimension_semantics=sem, vmem_limit_bytes=VMEM_LIMIT)


def _const_spec(shape):
    nd = len(shape)
    return pl.BlockSpec(shape, lambda *_: (0,) * nd, pipeline_mode=pl.Buffered(1))


def _dot(a, b):
    return jnp.dot(a, b, preferred_element_type=F32)


def _split_bf16(a):
    hi = a.astype(BF16)
    lo = (a - hi.astype(F32)).astype(BF16)
    return hi, lo


def _row_groups(ref, groups=2):
    n = ref.shape[0] // groups
    return [pl.ds(i * n, n) for i in range(groups)]


def _rms(x, gain):
    return x * lax.rsqrt(jnp.mean(x * x, axis=-1, keepdims=True) + NORM_EPS) * gain


def _ab_in_kernel(x_ref, ng_ref, wm_ref, wl_ref, wf_ref, up_ref, bias_ref,
                  q_ref, k_ref, v_ref, r_ref, gf_ref, gb_ref, f_ref):
    h = _rms(x_ref[...], ng_ref[...]).astype(BF16)
    low3 = _dot(h, wl_ref[...])
    y = _dot(h, wm_ref[...])
    q_ref[...] = (y[:, :GLA_QK] * (GLA_DK ** -0.5)).astype(BF16)
    k_ref[...] = y[:, GLA_QK:2 * GLA_QK].astype(BF16)
    v_ref[...] = y[:, 2 * GLA_QK:2 * GLA_QK + GLA_V].astype(BF16)
    r_ref[...] = y[:, 2 * GLA_QK + GLA_V:].astype(BF16)
    low_hi, low_lo = _split_bf16(low3)
    lane = lax.broadcasted_iota(jnp.int32, (1, 6 * GLA_RANK), 1)
    mid = (lane >= 2 * GLA_RANK) & (lane < 4 * GLA_RANK)
    up_hi, up_lo = _split_bf16(up_ref[...])
    z = _dot(jnp.where(mid, low_lo, low_hi), jnp.concatenate([up_hi, up_hi, up_lo], axis=0)) + bias_ref[...]
    g = (jnp.minimum(z, 0.0) - jnp.log(1.0 + jnp.exp(-jnp.abs(z)))) * (1.0 / GLA_GATE_NORM)
    gf_ref[...] = g[:, :GLA_QK]
    gb_ref[...] = g[:, GLA_QK:]
    f_ref[...] = _dot(h, wf_ref[...]).astype(BF16)


def _ab_in(x, ng, wm, wl, wf, up, bias, tm):
    t = x.shape[0]
    row = lambda w: pl.BlockSpec((tm, w), lambda i: (i, 0))
    outs = [(GLA_QK, BF16), (GLA_QK, BF16), (GLA_V, BF16), (GLA_V, BF16),
            (GLA_QK, F32), (GLA_QK, F32), (FNET_W, BF16)]
    return pl.pallas_call(
        _ab_in_kernel,
        grid=(t // tm,),
        in_specs=[row(D_MODEL)] + [_const_spec(a.shape) for a in (ng, wm, wl, wf, up, bias)],
        out_specs=[row(w) for w, _ in outs],
        out_shape=[jax.ShapeDtypeStruct((t, w), d) for w, d in outs],
        compiler_params=_params("parallel"),
        name="ab_in",
    )(x, ng, wm, wl, wf, up, bias)


def _gla_local(q_ref, k_ref, v_ref, g_ref, *, chunks, reverse):
    c = GLA_CHUNK
    n = chunks * c
    pair_w = 2 * GLA_DK
    ri = lax.broadcasted_iota(jnp.int32, (n, n), 0)
    ci = lax.broadcasted_iota(jnp.int32, (n, n), 1)
    causal = (ri <= ci) if reverse else (ri >= ci)
    tri = jnp.where(causal & ((ri // c) == (ci // c)), 1.0, 0.0).astype(BF16)
    g_hi, g_lo = _split_bf16(g_ref[0])
    bc = _dot(tri, g_hi) + _dot(tri, g_lo)
    q_in = (q_ref[0].astype(F32) * jnp.exp(bc)).astype(BF16)
    k_in = k_ref[0].astype(F32) * jnp.exp(-bc)
    v = v_ref[0]
    last = [j * c if reverse else (j + 1) * c - 1 for j in range(chunks)]
    tot = jnp.concatenate([bc[t:t + 1, :] for t in last] + [jnp.zeros((pair_w - chunks, GLA_QK), F32)],
                          axis=0)
    e_row = jnp.exp(tot)
    e_col = jnp.exp(tot.T)
    r2 = lax.broadcasted_iota(jnp.int32, (c, pair_w), 0)
    c2 = lax.broadcasted_iota(jnp.int32, (c, pair_w), 1) % c
    keep = (r2 <= c2) if reverse else (r2 >= c2)
    first = lax.broadcasted_iota(jnp.int32, (1, pair_w), 1) < GLA_DK
    rb = lax.broadcasted_iota(jnp.int32, (pair_w, 2 * GLA_DV), 0) < GLA_DK
    cb = lax.broadcasted_iota(jnp.int32, (pair_w, 2 * GLA_DV), 1) < GLA_DV
    diag = rb == cb
    local = {}
    for j in range(chunks):
        rows = slice(j * c, (j + 1) * c)
        k_c = k_in[rows]
        k_dec = (k_c * e_row[j:j + 1, :]).astype(BF16)
        for p in range(GLA_HEADS // 2):
            pair = slice(pair_w * p, pair_w * (p + 1))
            kp = k_c[:, pair]
            k_bd = jnp.concatenate([jnp.where(first, kp, 0.0), jnp.where(first, 0.0, kp)], axis=0).astype(BF16)
            qp = q_in[rows, pair]
            att = lax.dot_general(qp, k_bd, _NT, preferred_element_type=F32)
            att = jnp.where(keep, att, 0.0).astype(BF16)
            vp = v[rows, 2 * GLA_DV * p:2 * GLA_DV * (p + 1)]
            v_bd = jnp.where(diag, jnp.concatenate([vp, vp], axis=0), jnp.zeros((), BF16))
            upd = lax.dot_general(k_dec[:, pair], vp, _TN, preferred_element_type=F32)
            local[j, p] = (jnp.concatenate([att, qp], axis=1), v_bd, jnp.where(diag, upd, 0.0),
                           e_col[pair, j:j + 1])
    return local


def _gla_serial(local, o_ref, state_ref, *, chunks, reverse):
    c = GLA_CHUNK
    states = [state_ref[p] for p in range(GLA_HEADS // 2)]
    for j in (reversed(range(chunks)) if reverse else range(chunks)):
        outs = []
        for p in range(GLA_HEADS // 2):
            lhs, v_bd, upd, decay = local[j, p]
            outs.append(_dot(lhs, jnp.concatenate([v_bd, states[p].astype(BF16)], axis=0)))
            states[p] = states[p] * decay + upd
        o_ref[0, j * c:(j + 1) * c, :] = jnp.concatenate(outs, axis=-1)
    for p in range(GLA_HEADS // 2):
        state_ref[p] = states[p]


def _gla_kernel(qf_ref, kf_ref, vf_ref, gf_ref, qb_ref, kb_ref, vb_ref, gb_ref,
                of_ref, ob_ref, sf_ref, sb_ref, *, chunks):
    @pl.when(pl.program_id(1) == 0)
    def _():
        sf_ref[...] = jnp.zeros_like(sf_ref)
        sb_ref[...] = jnp.zeros_like(sb_ref)

    fwd = _gla_local(qf_ref, kf_ref, vf_ref, gf_ref, chunks=chunks, reverse=False)
    bwd = _gla_local(qb_ref, kb_ref, vb_ref, gb_ref, chunks=chunks, reverse=True)
    _gla_serial(fwd, of_ref, sf_ref, chunks=chunks, reverse=False)
    _gla_serial(bwd, ob_ref, sb_ref, chunks=chunks, reverse=True)


def _gla(q, k, v, gf, gb, blk):
    b, s, _ = q.shape
    nb = s // blk
    fwd = lambda w: pl.BlockSpec((1, blk, w), lambda i, n: (i, n, 0))
    bwd = lambda w: pl.BlockSpec((1, blk, w), lambda i, n: (i, nb - 1 - n, 0))
    state = pltpu.VMEM((GLA_HEADS // 2, 2 * GLA_DK, 2 * GLA_DV), F32)
    return pl.pallas_call(
        functools.partial(_gla_kernel, chunks=blk // GLA_CHUNK),
        grid=(b, nb),
        in_specs=[fwd(GLA_QK), fwd(GLA_QK), fwd(GLA_V), fwd(GLA_QK),
                  bwd(GLA_QK), bwd(GLA_QK), bwd(GLA_V), bwd(GLA_QK)],
        out_specs=[fwd(GLA_V), bwd(GLA_V)],
        out_shape=[jax.ShapeDtypeStruct((b, s, GLA_V), F32)] * 2,
        scratch_shapes=[state, state],
        compiler_params=_params("parallel", "arbitrary"),
        name="gla",
    )(q, k, v, gf, q, k, v, gb)


def _fnet_kernel(x_ref, ww_ref, k1c_ref, k1s_ref, k2c_ref, k2s_ref, twc_ref, tws_ref,
                 o_ref, z_ref, a_ref, *, nl, nm, groups):
    gw = FNET_GW
    cw = groups * gw
    rows_per = SUBLANES * nm
    for i in range(nl // SUBLANES):
        xs = x_ref[0, pl.ds(i * rows_per, rows_per), :]
        zr, zi = [], []
        for g in range(groups):
            zz = _dot(xs[:, g * gw:(g + 1) * gw], ww_ref[...])
            zr.append(zz[:, :gw])
            zi.append(zz[:, gw:])
        zcat = jnp.concatenate(zr + zi, axis=-1)
        z_ref[pl.ds(i * SUBLANES, SUBLANES)] = zcat.reshape(SUBLANES, nm // SUBLANES, SUBLANES, 2 * cw)
    for mb in range(nm // SUBLANES):
        z = z_ref[:, mb].reshape(nl * SUBLANES, 2 * cw)
        zr, zi = z[:, :cw], z[:, cw:]
        za = z.astype(BF16)
        zb = jnp.concatenate([zi, -zr], axis=-1).astype(BF16)
        a = _dot(k1c_ref[...], za) + _dot(k1s_ref[...], zb)
        ar, ai = a[:, :cw], a[:, cw:]
        tc = twc_ref[:, mb].reshape(nl * SUBLANES, gw)
        ts = tws_ref[:, mb].reshape(nl * SUBLANES, gw)
        tc = jnp.concatenate([tc] * groups, axis=-1)
        ts = jnp.concatenate([ts] * groups, axis=-1)
        pr = ar * tc + ai * ts
        pi = ai * tc - ar * ts
        a_ref[:, mb] = jnp.concatenate([pr, pi], axis=-1).reshape(nl, SUBLANES, 2 * cw)
    for ab in range(nl // SUBLANES):
        bb = a_ref[pl.ds(ab * SUBLANES, SUBLANES)].reshape(SUBLANES * nm, 2 * cw).astype(BF16)
        res = _dot(k2c_ref[...], bb[:, :cw]) + _dot(k2s_ref[...], bb[:, cw:])
        o_ref[0, :, ab] = res.reshape(nm, SUBLANES, cw)


def _dft_tables(s):
    nm = 64
    nl = s // nm
    assert nl * nm == s and nl % SUBLANES == 0
    eye = np.eye(SUBLANES)

    def dft(n):
        idx = np.arange(n)
        ang = 2.0 * np.pi * ((idx[:, None] * idx[None, :]) % n) / n
        return np.cos(ang), np.sin(ang)

    k1c, k1s = (np.kron(f, eye) for f in dft(nl))
    k2c, k2s = (np.einsum("nm,ij->nijm", f, eye).reshape(nm * SUBLANES, SUBLANES * nm)
                for f in dft(nm))
    a = np.arange(nl)[:, None]
    m = np.arange(nm)[None, :]
    ang = 2.0 * np.pi * ((a * m) % s) / s
    norm = 1.0 / np.sqrt(float(s) * FNET_GW)
    shape = (nl, nm // SUBLANES, SUBLANES, FNET_GW)
    twc = np.broadcast_to((np.cos(ang) * norm).reshape(nl, nm // SUBLANES, SUBLANES, 1), shape)
    tws = np.broadcast_to((np.sin(ang) * norm).reshape(nl, nm // SUBLANES, SUBLANES, 1), shape)
    w = np.arange(FNET_GW)
    angw = 2.0 * np.pi * ((w[:, None] * w[None, :]) % FNET_GW) / FNET_GW
    ww = np.concatenate([np.cos(angw), -np.sin(angw)], axis=1)
    bf = lambda v: jnp.asarray(v, dtype=F32).astype(BF16)
    return (nl, nm, bf(ww), bf(k1c), bf(k1s), bf(k2c), bf(k2s),
            jnp.asarray(twc, dtype=F32), jnp.asarray(tws, dtype=F32))


def _fnet(f, groups):
    b, s, _ = f.shape
    nl, nm, ww, k1c, k1s, k2c, k2s, twc, tws = _dft_tables(s)
    cw = groups * FNET_GW
    consts = (ww, k1c, k1s, k2c, k2s, twc, tws)
    out = pl.pallas_call(
        functools.partial(_fnet_kernel, nl=nl, nm=nm, groups=groups),
        grid=(b, FNET_W // cw),
        in_specs=[pl.BlockSpec((1, s, cw), lambda i, j: (i, 0, j))]
                 + [_const_spec(a.shape) for a in consts],
        out_specs=pl.BlockSpec((1, nm, nl // SUBLANES, SUBLANES, cw), lambda i, j: (i, 0, 0, 0, j)),
        out_shape=jax.ShapeDtypeStruct((b, nm, nl // SUBLANES, SUBLANES, FNET_W), F32),
        scratch_shapes=[pltpu.VMEM((nl, nm // SUBLANES, SUBLANES, 2 * cw), F32),
                        pltpu.VMEM((nl, nm // SUBLANES, SUBLANES, 2 * cw), F32)],
        compiler_params=_params("parallel", "parallel"),
        name="fnet",
    )(f, *consts)
    return out.reshape(b, s, FNET_W)


FFN_SPLIT = 6 * 256


def _swiglu(xs, ng_ref, wg_ref, wu_ref, wd_ref):
    hs = [_rms(x, ng_ref[...]).astype(BF16) for x in xs]
    ys = list(xs)
    for cols in (slice(0, FFN_SPLIT), slice(FFN_SPLIT, FFN_HIDDEN)):
        gu = [(_dot(h, wg_ref[:, cols]), _dot(h, wu_ref[:, cols])) for h in hs]
        acts = [(g * jax.nn.sigmoid(g) * u).astype(BF16) for g, u in gu]
        ys = [y + _dot(a, wd_ref[cols, :]) for y, a in zip(ys, acts)]
    return ys


def _ab_out_ffn_kernel(of_ref, ob_ref, r_ref, fo_ref, x_ref, og_ref, wo_ref, wf_ref,
                       ng_ref, wg_ref, wu_ref, wd_ref, y_ref):
    groups = _row_groups(y_ref)
    xs = []
    for rows in groups:
        o = of_ref[rows, :] + ob_ref[rows, :]
        heads = [_rms(o[:, GLA_DV * h:GLA_DV * (h + 1)], og_ref[...]) for h in range(GLA_HEADS)]
        r = r_ref[rows, :].astype(F32)
        o = jnp.concatenate(heads, axis=-1) * (r * jax.nn.sigmoid(r))
        mixed = _dot(o.astype(BF16), wo_ref[...]) + _dot(fo_ref[rows, :].astype(BF16), wf_ref[...])
        xs.append(x_ref[rows, :] + mixed)
    for rows, y in zip(groups, _swiglu(xs, ng_ref, wg_ref, wu_ref, wd_ref)):
        y_ref[rows, :] = y


def _c_out_ffn_kernel(o_ref, x_ref, w_ref, ng_ref, wg_ref, wu_ref, wd_ref, y_ref):
    groups = _row_groups(y_ref)
    xs = [x_ref[rows, :] + _dot(o_ref[rows, :], w_ref[...]) for rows in groups]
    for rows, y in zip(groups, _swiglu(xs, ng_ref, wg_ref, wu_ref, wd_ref)):
        y_ref[rows, :] = y


def _rows_call(body, name, row_args, const_args, tm):
    t = row_args[0].shape[0]
    row = lambda a: pl.BlockSpec((tm, a.shape[1]), lambda i: (i, 0))
    return pl.pallas_call(
        body,
        grid=(t // tm,),
        in_specs=[row(a) for a in row_args] + [_const_spec(a.shape) for a in const_args],
        out_specs=pl.BlockSpec((tm, D_MODEL), lambda i: (i, 0)),
        out_shape=jax.ShapeDtypeStruct((t, D_MODEL), F32),
        compiler_params=_params("parallel"),
        name=name,
    )(*row_args, *const_args)


SUM_ROWS = 16
_QUARTER = HEAD_DIM // 4
_HEAD_PERM = np.concatenate([np.arange(0, _QUARTER), np.arange(2 * _QUARTER, 3 * _QUARTER),
                             np.arange(_QUARTER, 2 * _QUARTER), np.arange(3 * _QUARTER, 4 * _QUARTER)])


def _qk_norm_rope(y2, ones_bd, gain, cos, sin):
    sq = (y2 * y2).astype(BF16)
    ms = _dot(sq, ones_bd) * (1.0 / HEAD_DIM)
    inv = lax.rsqrt(ms + NORM_EPS)
    halves = []
    for j in range(2):
        yh = y2[:, HEAD_DIM * j:HEAD_DIM * (j + 1)] * gain
        halves.append(yh * cos + pltpu.roll(yh, HEAD_DIM // 2, 1) * sin)
    return jnp.concatenate(halves, axis=-1) * inv


def _c_in_kernel(x_ref, ng_ref, w_ref, bd_ref, qg_ref, kg_ref, cos_ref, sin_ref, q_ref, k_ref, vt_ref):
    h = _rms(x_ref[...], ng_ref[...]).astype(BF16)
    ones_bd = bd_ref[...]
    cos, sin = cos_ref[...], sin_ref[...]
    pair_w = 2 * HEAD_DIM
    n_pairs = (ATTN_Q + 2 * ATTN_KV) // pair_w
    project = lambda i: _dot(h, w_ref[:, pair_w * i:pair_w * (i + 1)])
    y_next = project(0)
    for i in range(n_pairs):
        y = y_next
        if i + 1 < n_pairs:
            y_next = project(i + 1)
        if i < ATTN_HEADS // 2:
            cols = slice(pair_w * i, pair_w * (i + 1))
            q_ref[:, cols] = _qk_norm_rope(y, ones_bd, qg_ref[...], cos, sin).astype(BF16)
        elif i == ATTN_HEADS // 2:
            k_ref[...] = _qk_norm_rope(y, ones_bd, kg_ref[...], cos, sin).astype(BF16)
        else:
            for j in range(KV_HEADS):
                vt_ref[0, j, :HEAD_DIM, :] = y[:, HEAD_DIM * j:HEAD_DIM * (j + 1)].T.astype(BF16)
                vt_ref[0, j, HEAD_DIM:, :] = jnp.ones((SUM_ROWS, y.shape[0]), BF16)


def _rope_tables(s):
    pos = np.arange(s)
    half = HEAD_DIM // 2
    inv = ROPE_THETA ** (-np.arange(0, half, 2, dtype=np.float64) / half)
    ar = (pos // GRID_W)[:, None] * inv[None, :]
    ac = (pos % GRID_W)[:, None] * inv[None, :]
    ang = np.concatenate([ar, ac, ar, ac], axis=-1)
    sign = np.where(np.arange(HEAD_DIM) < half, -1.0, 1.0)[None, :]
    return jnp.asarray(np.cos(ang), dtype=F32), jnp.asarray(np.sin(ang) * sign, dtype=F32)


def _c_in(x, ng, w, qg, kg, s, tm):
    t = x.shape[0]
    per_seq = s // tm
    row = lambda wd: pl.BlockSpec((tm, wd), lambda i: (i, 0))
    tab = pl.BlockSpec((tm, HEAD_DIM), lambda i: (i % per_seq, 0))
    ones_bd = jnp.asarray(np.kron(np.eye(2), np.ones((HEAD_DIM, HEAD_DIM))), dtype=BF16)
    consts = (ng, w, ones_bd, qg, kg)
    vt_rows = HEAD_DIM + SUM_ROWS
    return pl.pallas_call(
        _c_in_kernel,
        grid=(t // tm,),
        in_specs=[row(D_MODEL)] + [_const_spec(a.shape) for a in consts] + [tab] * 2,
        out_specs=[row(ATTN_Q), row(ATTN_KV),
                   pl.BlockSpec((1, KV_HEADS, vt_rows, tm), lambda i: (i // per_seq, 0, 0, i % per_seq))],
        out_shape=[jax.ShapeDtypeStruct((t, ATTN_Q), BF16), jax.ShapeDtypeStruct((t, ATTN_KV), BF16),
                   jax.ShapeDtypeStruct((t // s, KV_HEADS, vt_rows, s), BF16)],
        compiler_params=_params("parallel"),
        name="c_in",
    )(x, *consts, *_rope_tables(s))


def _attn_kernel(q_ref, k_ref, vt_ref, o_ref, s_ref, m_ref, *, tq):
    group = ATTN_HEADS // KV_HEADS
    nblk = q_ref.shape[1] // tq

    def rows_of(r):
        return pl.ds(pl.multiple_of(r * tq, tq), tq)

    def scores(r, g):
        q = q_ref[0, rows_of(r), HEAD_DIM * g:HEAD_DIM * (g + 1)]
        s_ref[g] = lax.dot_general(k_ref[0], q, _NT, preferred_element_type=F32)

    def col_max(g):
        m_ref[g] = jnp.max(s_ref[g], axis=0, keepdims=True)

    def finish(r, g):
        ov = _dot(vt_ref[0, 0], jnp.exp2(s_ref[g] - m_ref[g]).astype(BF16))
        out = ov[:HEAD_DIM] / ov[HEAD_DIM:HEAD_DIM + 1]
        o_ref[0, rows_of(r), HEAD_DIM * g:HEAD_DIM * (g + 1)] = out.T.astype(o_ref.dtype)

    scores(0, 0)
    scores(0, 1)
    col_max(0)

    def body(r, carry):
        nxt = jnp.minimum(r + 1, nblk - 1)
        for g in range(group):
            scores(r if g + 2 < group else nxt, (g + 2) % group)
            col_max((g + 1) % group)
            finish(r, g)
        return carry

    lax.fori_loop(0, nblk, body, 0, unroll=4)


def _attn(q, k, vt, tq):
    b, s, _ = q.shape
    gw = ATTN_Q // KV_HEADS
    return pl.pallas_call(
        functools.partial(_attn_kernel, tq=tq),
        grid=(b, KV_HEADS),
        in_specs=[pl.BlockSpec((1, s, gw), lambda i, j: (i, 0, j)),
                  pl.BlockSpec((1, s, HEAD_DIM), lambda i, j: (i, 0, j)),
                  pl.BlockSpec((1, 1, HEAD_DIM + SUM_ROWS, s), lambda i, j: (i, j, 0, 0))],
        out_specs=pl.BlockSpec((1, s, gw), lambda i, j: (i, 0, j)),
        out_shape=jax.ShapeDtypeStruct((b, s, ATTN_Q), BF16),
        scratch_shapes=[pltpu.VMEM((ATTN_HEADS // KV_HEADS, s, tq), F32),
                        pltpu.VMEM((ATTN_HEADS // KV_HEADS, 1, tq), F32)],
        compiler_params=_params("parallel", "parallel"),
        name="attn",
    )(q, k, vt)


def _prep_weights(ab_norm, ab_w_in, gla_up_f, gla_bias_f, gla_up_b, gla_bias_b, gla_out_norm,
                  ab_w_out, c_norm, c_w_in, c_q_norm, c_k_norm, c_w_out, ffn_norm,
                  ffn_w_gate, ffn_w_up, ffn_w_down):
    n_main = 2 * GLA_QK + 2 * GLA_V
    w_in = ab_w_in[0]
    zeros = jnp.zeros((GLA_RANK, GLA_QK), F32)
    up = jnp.concatenate([jnp.concatenate([gla_up_f[0], zeros], axis=1),
                          jnp.concatenate([zeros, gla_up_b[0]], axis=1)], axis=0)
    qk_w = ATTN_Q + ATTN_KV
    c_w = c_w_in[0].astype(BF16)
    c_qk = c_w[:, :qk_w].reshape(D_MODEL, qk_w // HEAD_DIM, 2, 2, _QUARTER).swapaxes(2, 3)
    c_w = jnp.concatenate([c_qk.reshape(D_MODEL, qk_w), c_w[:, qk_w:]], axis=1)
    return dict(
        ab_ng=ab_norm[0][None, :],
        ab_wm=w_in[:, :n_main].astype(BF16),
        ab_wl=jnp.tile(w_in[:, n_main:n_main + 2 * GLA_RANK].astype(BF16), (1, 3)),
        ab_wf=w_in[:, n_main + 2 * GLA_RANK:].astype(BF16),
        ab_up=up,
        ab_bias=jnp.concatenate([gla_bias_f[0], gla_bias_b[0]])[None, :],
        ab_og=gla_out_norm[0][None, :],
        ab_wo=ab_w_out[0][:GLA_V].astype(BF16),
        ab_wfo=ab_w_out[0][GLA_V:].astype(BF16),
        c_ng=c_norm[0][None, :],
        c_w=c_w,
        c_qg=(c_q_norm[0][_HEAD_PERM] * ((HEAD_DIM ** -0.5) * float(np.log2(np.e))))[None, :],
        c_kg=c_k_norm[0][_HEAD_PERM][None, :],
        c_wo=c_w_out[0].astype(BF16),
        ffn_ng=[ffn_norm[i][None, :] for i in range(2)],
        ffn_wg=[ffn_w_gate[i].astype(BF16) for i in range(2)],
        ffn_wu=[ffn_w_up[i].astype(BF16) for i in range(2)],
        ffn_wd=[ffn_w_down[i].astype(BF16) for i in range(2)],
    )


def _trunk(x, w, tm=1024, tm_ffn=512, gla_blk=256, tq=256, fnet_groups=2):
    b, s, d = x.shape
    t = b * s
    xf = x.reshape(t, d)
    q, k, v, r, gf, gb, f = _ab_in(xf, w["ab_ng"], w["ab_wm"], w["ab_wl"], w["ab_wf"],
                                   w["ab_up"], w["ab_bias"], tm)
    seq = lambda a: a.reshape(b, s, a.shape[-1])
    o_f, o_b = _gla(seq(q), seq(k), seq(v), seq(gf), seq(gb), gla_blk)
    fo = _fnet(seq(f), fnet_groups)
    ffn = lambda i: (w["ffn_ng"][i], w["ffn_wg"][i], w["ffn_wu"][i], w["ffn_wd"][i])
    xf = _rows_call(_ab_out_ffn_kernel, "ab_out_ffn",
                    (o_f.reshape(t, GLA_V), o_b.reshape(t, GLA_V), r, fo.reshape(t, FNET_W), xf),
                    (w["ab_og"], w["ab_wo"], w["ab_wfo"]) + ffn(0), tm_ffn)
    q, k, vt = _c_in(xf, w["c_ng"], w["c_w"], w["c_qg"], w["c_kg"], s, tm)
    o = _attn(seq(q), seq(k), vt, tq)
    xf = _rows_call(_c_out_ffn_kernel, "c_out_ffn", (o.reshape(t, ATTN_Q), xf), (w["c_wo"],) + ffn(1), tm_ffn)
    return xf.reshape(b, s, d)


def kernel(x_prompt, x_sample, ab_norm, ab_w_in, gla_up_f, gla_bias_f, gla_up_b, gla_bias_b,
           gla_out_norm, ab_w_out, c_norm, c_w_in, c_q_norm, c_k_norm, c_w_out, ffn_norm,
           ffn_w_gate, ffn_w_up, ffn_w_down):
    w = _prep_weights(ab_norm, ab_w_in, gla_up_f, gla_bias_f, gla_up_b, gla_bias_b, gla_out_norm,
                      ab_w_out, c_norm, c_w_in, c_q_norm, c_k_norm, c_w_out, ffn_norm,
                      ffn_w_gate, ffn_w_up, ffn_w_down)
    return _trunk(x_prompt, w), _trunk(x_sample, w)
```
